```python
import math, functools
import jax
import jax.numpy as jnp
from jax import lax
import numpy as np

D_MODEL = 2048
BATCH = 4
SEQ = 2048
DEPTH = 2
DEC_BATCH = 128
DEC_SEQ = 8
PAST_LEN = 8192
PAGE_SIZE = 128

SSD_D_INNER = D_MODEL
SSD_HEAD_DIM = 64
SSD_HEADS = SSD_D_INNER // SSD_HEAD_DIM
SSD_GROUPS = 8
SSD_STATE = 128
SSD_CONV = 4
SSD_CHUNK = 128
SSD_CONV_DIM = SSD_D_INNER + 2 * SSD_GROUPS * SSD_STATE
SB_HEAD_DIM = 128
SB_HEADS = D_MODEL // SB_HEAD_DIM
SB_KV_HEADS = 4
SB_Q_PER_KV = SB_HEADS // SB_KV_HEADS
SB_BLOCK = 128
SB_SCALE = SB_HEAD_DIM ** -0.5
MLA_HEADS = 16
MLA_Q_RANK = 512
MLA_KV_RANK = 512
MLA_NOPE = 128
MLA_ROPE = 64
MLA_V = 128
MLA_BLOCK = 128
MLA_SCALE = (MLA_NOPE + MLA_ROPE) ** -0.5
ROPE_THETA = 10000.0
CMLP_DIM = D_MODEL
CMLP_GROUPS = 8
CMLP_CHUNK = 128
FFN_DIM = 4 * D_MODEL
EPS = 1e-6
F32 = jnp.float32

A_Z = SSD_D_INNER
A_XBC = A_Z + SSD_CONV_DIM
A_DT = A_XBC + SSD_HEADS
A_Q = A_DT + SB_HEADS * SB_HEAD_DIM
A_K = A_Q + SB_KV_HEADS * SB_HEAD_DIM
A_IN = A_K + SB_KV_HEADS * SB_HEAD_DIM
A_MIX = SSD_D_INNER + SB_HEADS * SB_HEAD_DIM
C_CQ = MLA_Q_RANK
C_CKV = C_CQ + MLA_KV_RANK
C_KPE = C_CKV + MLA_ROPE
C_U = C_KPE + CMLP_DIM
C_IN = C_U + CMLP_DIM
C_MIX = MLA_HEADS * MLA_V + CMLP_DIM

kernel_name = 'hybrid_ssd_stickbreak_mla_chunkmlp_step'


def rms_norm(x, g):
    xf = x.astype(F32)
    y = xf * lax.rsqrt(jnp.mean(xf * xf, axis=-1, keepdims=True) + EPS)
    return (y * g.astype(F32)).astype(x.dtype)


def layer_norm(x, g, b):
    xf = x.astype(F32)
    xc = xf - jnp.mean(xf, axis=-1, keepdims=True)
    y = xc * lax.rsqrt(jnp.mean(xc * xc, axis=-1, keepdims=True) + EPS)
    return (y * g.astype(F32) + b.astype(F32)).astype(x.dtype)


def group_rms_norm(y, g, groups):
    b, l, d = y.shape
    yg = y.astype(F32).reshape(b, l, groups, d // groups)
    yg = yg * lax.rsqrt(jnp.mean(yg * yg, axis=-1, keepdims=True) + EPS)
    return yg.reshape(b, l, d) * g.astype(F32)


def rope(x, pos):
    half = x.shape[-1] // 2
    inv = ROPE_THETA ** (-jnp.arange(half, dtype=F32) / half)
    ang = pos.astype(F32)[:, None] * inv[None, :]
    cos = jnp.cos(ang)[:, None, :]
    sin = jnp.sin(ang)[:, None, :]
    x1 = x[..., :half].astype(F32)
    x2 = x[..., half:].astype(F32)
    return jnp.concatenate([x1 * cos - x2 * sin, x2 * cos + x1 * sin], axis=-1).astype(x.dtype)


def to_blocks(x, blk):
    b, t = x.shape[0], x.shape[1]
    return jnp.moveaxis(x.reshape(b, t // blk, blk, *x.shape[2:]), 1, 0)


def from_blocks(x):
    nb, b, blk = x.shape[0], x.shape[1], x.shape[2]
    return jnp.moveaxis(x, 0, 1).reshape(b, nb * blk, *x.shape[3:])


def causal_dwconv(xbc, buf, w, bias):
    full = jnp.concatenate([buf.astype(xbc.dtype), xbc], axis=1)
    out = lax.conv_general_dilated(full, w.astype(full.dtype)[:, None, :], window_strides=(1,), padding='VALID',
                                   dimension_numbers=('NWC', 'WIO', 'NWC'), feature_group_count=full.shape[-1])
    return out + bias.astype(full.dtype), full[:, full.shape[1] - (SSD_CONV - 1):]


def ssd_scan(x, dt, a_neg, bm, cm, h0):
    b, l, h, p = x.shape
    g, n = bm.shape[2], bm.shape[3]
    r = h // g
    q = min(SSD_CHUNK, l)
    c = l // q
    x = x.astype(F32).reshape(b, c, q, g, r, p)
    dt = dt.astype(F32).reshape(b, c, q, g, r)
    bm = bm.astype(F32).reshape(b, c, q, g, n)
    cm = cm.astype(F32).reshape(b, c, q, g, n)
    at = jnp.moveaxis(jnp.cumsum(dt * a_neg.astype(F32).reshape(g, r), axis=2), 2, -1)
    mask = jnp.tril(jnp.ones((q, q), dtype=bool))
    decay = jnp.exp(jnp.where(mask, at[..., :, None] - at[..., None, :], -jnp.inf))
    cb = jnp.einsum('bcign,bcjgn->bcgij', cm, bm)
    xdt = x * dt[..., None]
    y_diag = jnp.einsum('bcgrij,bcjgrp->bcigrp', cb[:, :, :, None] * decay, xdt)
    decay_to_end = jnp.exp(at[..., -1:] - at)
    s_chunk = jnp.einsum('bcjgn,bcgrj,bcjgrp->bcgrpn', bm, decay_to_end, xdt)
    chunk_decay = jnp.exp(at[..., -1])

    def step(s, inp):
        s_c, d_c = inp
        return s * d_c[..., None, None] + s_c, s

    s_final, s_in = lax.scan(step, h0.astype(F32).reshape(b, g, r, p, n),
                             (jnp.moveaxis(s_chunk, 1, 0), jnp.moveaxis(chunk_decay, 1, 0)))
    s_in = jnp.moveaxis(s_in, 0, 1)
    y_off = jnp.einsum('bcign,bcgri,bcgrpn->bcigrp', cm, jnp.exp(at), s_in)
    return (y_diag + y_off).reshape(b, l, h, p), s_final.reshape(b, h, p, n)


def stick_breaking(q, k, v, q_pos, k_pos):
    z = jnp.einsum('btkgd,bskd->bkgts', q, k).astype(F32) * SB_SCALE
    vis = k_pos[None, :] < q_pos[:, None]
    log_fail = jnp.where(vis, jax.nn.log_sigmoid(-z), 0.0)
    later = lax.cumsum(log_fail, axis=z.ndim - 1, reverse=True) - log_fail
    w = jnp.where(vis, jnp.exp(jax.nn.log_sigmoid(z) + later), 0.0)
    return jnp.einsum('bkgts,bskd->btkgd', w.astype(v.dtype), v)


def sb_prompt(q, k, v):
    t = q.shape[1]
    k_pos = jnp.arange(t)

    def block(args):
        qb, i = args
        return stick_breaking(qb, k, v, i * SB_BLOCK + jnp.arange(SB_BLOCK), k_pos)

    return from_blocks(lax.map(block, (to_blocks(q, SB_BLOCK), jnp.arange(t // SB_BLOCK))))


def sb_sample(q, k_new, v_new, cache_k, cache_v, page_table, layer):
    t = q.shape[1]
    past = page_table.shape[1] * PAGE_SIZE
    k_pos = jnp.arange(past + t)
    q_pos = past + jnp.arange(t)

    def one(args):
        qi, kn, vn, pt = args
        kk = jnp.concatenate([cache_k[layer, pt].reshape(past, SB_KV_HEADS, SB_HEAD_DIM).astype(kn.dtype), kn], axis=0)
        vv = jnp.concatenate([cache_v[layer, pt].reshape(past, SB_KV_HEADS, SB_HEAD_DIM).astype(vn.dtype), vn], axis=0)
        return stick_breaking(qi[None], kk[None], vv[None], q_pos, k_pos)[0]

    return lax.map(one, (q, k_new, v_new, page_table))


def mla_attend(q_nope, q_pe, k_nope, ckv, kpe, w_uv, q_pos, k_pos):
    s = jnp.einsum('bthd,bshd->bhts', q_nope, k_nope) + jnp.einsum('bthr,bsr->bhts', q_pe, kpe)
    s = jnp.where(k_pos[None, :] <= q_pos[:, None], s.astype(F32) * MLA_SCALE, -jnp.inf)
    p = jax.nn.softmax(s, axis=-1).astype(ckv.dtype)
    o_lat = jnp.einsum('bhts,bsc->bthc', p, ckv)
    return jnp.einsum('bthc,chd->bthd', o_lat, w_uv)


def mla_prompt(q_nope, q_pe, ckv, kpe, w_uk, g_kn, w_uv):
    t = q_nope.shape[1]
    k_nope = rms_norm(jnp.einsum('bsc,chd->bshd', ckv, w_uk), g_kn)
    k_pos = jnp.arange(t)

    def block(args):
        qn, qp, i = args
        return mla_attend(qn, qp, k_nope, ckv, kpe, w_uv, i * MLA_BLOCK + jnp.arange(MLA_BLOCK), k_pos)

    return from_blocks(lax.map(block, (to_blocks(q_nope, MLA_BLOCK), to_blocks(q_pe, MLA_BLOCK),
                                       jnp.arange(t // MLA_BLOCK))))


def mla_sample(q_nope, q_pe, ckv_new, kpe_new, cache_ckv, cache_kpe, page_table, layer, w_uk, g_kn, w_uv):
    t = q_nope.shape[1]
    past = page_table.shape[1] * PAGE_SIZE
    k_pos = jnp.arange(past + t)
    q_pos = past + jnp.arange(t)

    def one(args):
        qn, qp, cn, kn, pt = args
        ckv = jnp.concatenate([cache_ckv[layer, pt].reshape(past, MLA_KV_RANK).astype(cn.dtype), cn], axis=0)
        kpe = jnp.concatenate([cache_kpe[layer, pt].reshape(past, MLA_ROPE).astype(kn.dtype), kn], axis=0)
        k_nope = rms_norm(jnp.einsum('sc,chd->shd', ckv, w_uk), g_kn)
        return mla_attend(qn[None], qp[None], k_nope[None], ckv[None], kpe[None], w_uv, q_pos, k_pos)[0]

    return lax.map(one, (q_nope, q_pe, ckv_new, kpe_new, page_table))


def chunk_mlp(u, v, w_s, b_s):
    b, l, d = v.shape
    cl = min(CMLP_CHUNK, l)
    vg = v.reshape(b, l // cl, cl, CMLP_GROUPS, d // CMLP_GROUPS)
    w = jnp.tril(w_s[:, :cl, :cl]).astype(v.dtype)
    bias = jnp.transpose(b_s[:, :cl])[None, None, :, :, None].astype(v.dtype)
    mixed = jnp.einsum('gij,bcjgd->bcigd', w, vg) + bias
    return u * mixed.reshape(b, l, d)


def mixer_a(h, conv_buf, ssm0, attend, w_in, w_conv, b_conv, dt_bias, a_log, d_skip, g_ssd, w_out):
    b, l, _ = h.shape
    proj = h @ w_in
    z, xbc, dt_raw, q, k, v = jnp.split(proj, [A_Z, A_XBC, A_DT, A_Q, A_K], axis=-1)
    xbc, conv_new = causal_dwconv(xbc, conv_buf, w_conv, b_conv)
    xbc = jax.nn.silu(xbc)
    xs, bm, cm = jnp.split(xbc, [SSD_D_INNER, SSD_D_INNER + SSD_GROUPS * SSD_STATE], axis=-1)
    xs = xs.reshape(b, l, SSD_HEADS, SSD_HEAD_DIM)
    dt = jax.nn.softplus((dt_raw + dt_bias).astype(F32))
    y, ssm_new = ssd_scan(xs, dt, -jnp.exp(a_log.astype(F32)), bm.reshape(b, l, SSD_GROUPS, SSD_STATE),
                          cm.reshape(b, l, SSD_GROUPS, SSD_STATE), ssm0)
    y = y + d_skip.astype(F32)[:, None] * xs.astype(F32)
    y = y.reshape(b, l, SSD_D_INNER) * jax.nn.silu(z.astype(F32))
    y = group_rms_norm(y, g_ssd, SSD_GROUPS).astype(h.dtype)
    q = q.reshape(b, l, SB_KV_HEADS, SB_Q_PER_KV, SB_HEAD_DIM)
    k = k.reshape(b, l, SB_KV_HEADS, SB_HEAD_DIM)
    v = v.reshape(b, l, SB_KV_HEADS, SB_HEAD_DIM)
    o = attend(q, k, v).reshape(b, l, SB_HEADS * SB_HEAD_DIM).astype(h.dtype)
    out = jnp.concatenate([y, o], axis=-1) @ w_out
    return out, k, v, conv_new, ssm_new


def mixer_c(h, pos, attend, w_in, g_cq, g_ckv, w_uq, g_qn, g_qr, g_kr, g_v, b_v, w_s, b_s, w_out):
    b, l, _ = h.shape
    proj = h @ w_in
    cq, ckv, kpe, u, v = jnp.split(proj, [C_CQ, C_CKV, C_KPE, C_U], axis=-1)
    q = (rms_norm(cq, g_cq) @ w_uq).reshape(b, l, MLA_HEADS, MLA_NOPE + MLA_ROPE)
    q_nope = rms_norm(q[..., :MLA_NOPE], g_qn)
    q_pe = rope(rms_norm(q[..., MLA_NOPE:], g_qr), pos)
    ckv = rms_norm(ckv, g_ckv)
    kpe = rope(rms_norm(kpe, g_kr)[:, :, None], pos)[:, :, 0]
    o_mla = attend(q_nope, q_pe, ckv, kpe).reshape(b, l, MLA_HEADS * MLA_V).astype(h.dtype)
    u = jax.nn.gelu(u)
    v = layer_norm(jax.nn.gelu(v), g_v, b_v)
    o_cm = chunk_mlp(u, v, w_s, b_s).astype(h.dtype)
    out = jnp.concatenate([o_mla, o_cm], axis=-1) @ w_out
    return out, ckv, kpe, v


def sq_relu_ffn(h, w1, w2):
    return jnp.square(jax.nn.relu(h @ w1)) @ w2


def setup_inputs(seed: int = 0) -> dict:
    key = jax.random.key(seed)
    keys = list(jax.random.split(key, 48))

    def nk():
        return keys.pop()

    def nrm(shape, scale):
        return scale * jax.random.normal(nk(), shape, F32)

    def gain(shape):
        return 1.0 + 0.05 * jax.random.normal(nk(), shape, F32)

    n_a = (DEPTH + 1) // 2
    n_c = DEPTH // 2
    n_pages = PAST_LEN // PAGE_SIZE
    n_pool = (5 * DEC_BATCH * n_pages) // 4
    page_table = jax.random.permutation(nk(), n_pool)[: DEC_BATCH * n_pages].reshape(DEC_BATCH, n_pages).astype(jnp.int32)
    dt0 = jnp.exp(jax.random.uniform(nk(), (n_a, SSD_HEADS), F32, math.log(1e-3), math.log(1e-1)))
    dt_bias = dt0 + jnp.log(-jnp.expm1(-dt0))
    a_log = jnp.log(jax.random.uniform(nk(), (n_a, SSD_HEADS), F32, 1.0, 16.0))
    return {
        'x_prompt': nrm((BATCH, SEQ, D_MODEL), 1.0),
        'x_sample': nrm((DEC_BATCH, DEC_SEQ, D_MODEL), 1.0),
        'cache_sb_k': nrm((n_a, n_pool, PAGE_SIZE, SB_KV_HEADS, SB_HEAD_DIM), 1.0),
        'cache_sb_v': nrm((n_a, n_pool, PAGE_SIZE, SB_KV_HEADS, SB_HEAD_DIM), 1.0),
        'cache_mla_ckv': nrm((n_c, n_pool, PAGE_SIZE, MLA_KV_RANK), 1.0),
        'cache_mla_kpe': nrm((n_c, n_pool, PAGE_SIZE, MLA_ROPE), 1.0),
        'state_ssm': nrm((n_a, DEC_BATCH, SSD_HEADS, SSD_HEAD_DIM, SSD_STATE), SSD_STATE ** -0.5),
        'state_conv': nrm((n_a, DEC_BATCH, SSD_CONV - 1, SSD_CONV_DIM), 1.0),
        'page_table': page_table,
        'norm_mix': gain((DEPTH, D_MODEL)),
        'norm_ffn': gain((DEPTH, D_MODEL)),
        'w_in_a': nrm((n_a, D_MODEL, A_IN), D_MODEL ** -0.5),
        'w_conv': nrm((n_a, SSD_CONV, SSD_CONV_DIM), SSD_CONV ** -0.5),
        'b_conv': nrm((n_a, SSD_CONV_DIM), 0.02),
        'dt_bias': dt_bias,
        'a_log': a_log,
        'd_skip': gain((n_a, SSD_HEADS)),
        'g_ssd': gain((n_a, SSD_D_INNER)),
        'w_out_a': nrm((n_a, A_MIX, D_MODEL), A_MIX ** -0.5),
        'w_in_c': nrm((n_c, D_MODEL, C_IN), D_MODEL ** -0.5),
        'g_cq': gain((n_c, MLA_Q_RANK)),
        'g_ckv': gain((n_c, MLA_KV_RANK)),
        'w_uq': nrm((n_c, MLA_Q_RANK, MLA_HEADS * (MLA_NOPE + MLA_ROPE)), MLA_Q_RANK ** -0.5),
        'g_qn': gain((n_c, MLA_NOPE)),
        'g_qr': gain((n_c, MLA_ROPE)),
        'g_kn': gain((n_c, MLA_NOPE)),
        'g_kr': gain((n_c, MLA_ROPE)),
        'w_uk': nrm((n_c, MLA_KV_RANK, MLA_HEADS, MLA_NOPE), MLA_KV_RANK ** -0.5),
        'w_uv': nrm((n_c, MLA_KV_RANK, MLA_HEADS, MLA_V), MLA_KV_RANK ** -0.5),
        'g_v': gain((n_c, CMLP_DIM)),
        'b_v': nrm((n_c, CMLP_DIM), 0.02),
        'w_s': nrm((n_c, CMLP_GROUPS, CMLP_CHUNK, CMLP_CHUNK), CMLP_CHUNK ** -0.5),
        'b_s': gain((n_c, CMLP_GROUPS, CMLP_CHUNK)),
        'w_out_c': nrm((n_c, C_MIX, D_MODEL), C_MIX ** -0.5),
        'w_ffn1': nrm((DEPTH, D_MODEL, FFN_DIM), D_MODEL ** -0.5),
        'w_ffn2': nrm((DEPTH, FFN_DIM, D_MODEL), FFN_DIM ** -0.5),
    }


def reference(x_prompt, x_sample, cache_sb_k, cache_sb_v, cache_mla_ckv, cache_mla_kpe, state_ssm, state_conv,
              page_table, norm_mix, norm_ffn, w_in_a, w_conv, b_conv, dt_bias, a_log, d_skip, g_ssd, w_out_a,
              w_in_c, g_cq, g_ckv, w_uq, g_qn, g_qr, g_kn, g_kr, w_uk, w_uv, g_v, b_v, w_s, b_s, w_out_c,
              w_ffn1, w_ffn2):
    xp, xs = x_prompt, x_sample
    bp, tp = xp.shape[0], xp.shape[1]
    past = page_table.shape[1] * PAGE_SIZE
    pos_p = jnp.arange(tp)
    pos_s = past + jnp.arange(xs.shape[1])
    sbk_p, sbv_p, ssm_p, conv_p, ckv_p, kpe_p = [], [], [], [], [], []
    sbk_s, sbv_s, ssm_s, conv_s, ckv_s, kpe_s, cv_s = [], [], [], [], [], [], []
    for l in range(DEPTH):
        i = l // 2
        hp = rms_norm(xp, norm_mix[l])
        hs = rms_norm(xs, norm_mix[l])
        if l % 2 == 0:
            wa = (w_in_a[i], w_conv[i], b_conv[i], dt_bias[i], a_log[i], d_skip[i], g_ssd[i], w_out_a[i])
            conv0 = jnp.zeros((bp, SSD_CONV - 1, SSD_CONV_DIM), xp.dtype)
            ssm0 = jnp.zeros((bp, SSD_HEADS, SSD_HEAD_DIM, SSD_STATE), F32)
            mp, k_p, v_p, c_p, s_p = mixer_a(hp, conv0, ssm0, sb_prompt, *wa)
            attend_s = functools.partial(sb_sample, cache_k=cache_sb_k, cache_v=cache_sb_v,
                                         page_table=page_table, layer=i)
            ms, k_s, v_s, c_s, s_s = mixer_a(hs, state_conv[i], state_ssm[i], attend_s, *wa)
            sbk_p.append(k_p); sbv_p.append(v_p); conv_p.append(c_p); ssm_p.append(s_p)
            sbk_s.append(k_s); sbv_s.append(v_s); conv_s.append(c_s); ssm_s.append(s_s)
        else:
            wc = (w_in_c[i], g_cq[i], g_ckv[i], w_uq[i], g_qn[i], g_qr[i], g_kr[i], g_v[i], b_v[i], w_s[i],
                  b_s[i], w_out_c[i])
            attend_p = functools.partial(mla_prompt, w_uk=w_uk[i], g_kn=g_kn[i], w_uv=w_uv[i])
            mp, c_p, kr_p, _ = mixer_c(hp, pos_p, attend_p, *wc)
            attend_s = functools.partial(mla_sample, cache_ckv=cache_mla_ckv, cache_kpe=cache_mla_kpe,
                                         page_table=page_table, layer=i, w_uk=w_uk[i], g_kn=g_kn[i], w_uv=w_uv[i])
            ms, c_s, kr_s, v_s = mixer_c(hs, pos_s, attend_s, *wc)
            ckv_p.append(c_p); kpe_p.append(kr_p)
            ckv_s.append(c_s); kpe_s.append(kr_s); cv_s.append(v_s)
        xp = xp + mp.astype(xp.dtype)
        xs = xs + ms.astype(xs.dtype)
        xp = xp + sq_relu_ffn(rms_norm(xp, norm_ffn[l]), w_ffn1[l], w_ffn2[l]).astype(xp.dtype)
        xs = xs + sq_relu_ffn(rms_norm(xs, norm_ffn[l]), w_ffn1[l], w_ffn2[l]).astype(xs.dtype)
    return (xp, xs,
            jnp.stack(sbk_p), jnp.stack(sbv_p), jnp.stack(ssm_p), jnp.stack(conv_p), jnp.stack(ckv_p), jnp.stack(kpe_p),
            jnp.stack(sbk_s), jnp.stack(sbv_s), jnp.stack(ssm_s), jnp.stack(conv_s), jnp.stack(ckv_s), jnp.stack(kpe_s),
            jnp.stack(cv_s))
```

```python
import functools

import jax
import jax.numpy as jnp
from jax import lax
from jax.experimental import pallas as pl
from jax.experimental.pallas import tpu as pltpu

F32 = jnp.float32
BF16 = jnp.bfloat16

D_MODEL = 2048
PAGE_SIZE = 128
SSD_D_INNER = D_MODEL
SSD_HEAD_DIM = 64
SSD_HEADS = SSD_D_INNER // SSD_HEAD_DIM
SSD_GROUPS = 8
SSD_STATE = 128
SSD_CONV = 4
SSD_CHUNK = 128
SSD_CONV_DIM = SSD_D_INNER + 2 * SSD_GROUPS * SSD_STATE
SB_HEAD_DIM = 128
SB_HEADS = D_MODEL // SB_HEAD_DIM
SB_KV_HEADS = 4
SB_Q_PER_KV = SB_HEADS // SB_KV_HEADS
SB_SCALE = SB_HEAD_DIM ** -0.5
MLA_HEADS = 16
MLA_Q_RANK = 512
MLA_KV_RANK = 512
MLA_NOPE = 128
MLA_ROPE = 64
MLA_V = 128
MLA_SCALE = (MLA_NOPE + MLA_ROPE) ** -0.5
ROPE_THETA = 10000.0
CMLP_DIM = D_MODEL
CMLP_GROUPS = 8
CMLP_CHUNK = 128
FFN_DIM = 4 * D_MODEL
EPS = 1e-6

A_Z = SSD_D_INNER
A_XBC = A_Z + SSD_CONV_DIM
A_DT = A_XBC + SSD_HEADS
A_Q = A_DT + SB_HEADS * SB_HEAD_DIM
A_K = A_Q + SB_KV_HEADS * SB_HEAD_DIM
C_CQ = MLA_Q_RANK
C_CKV = C_CQ + MLA_KV_RANK
C_KPE = C_CKV + MLA_ROPE
C_U = C_KPE + CMLP_DIM

LANES = 128
ROW_TILE = 128
VMEM_LIMIT = 56 * 1024 * 1024


def _params(semantics, vmem=VMEM_LIMIT):
    return pltpu.CompilerParams(dimension_semantics=semantics, vmem_limit_bytes=vmem)


def _split3(x):
    hi = x.astype(BF16)
    r1 = x - hi.astype(F32)
    mid = r1.astype(BF16)
    lo = (r1 - mid.astype(F32)).astype(BF16)
    return hi, mid, lo


def _dot(a, b):
    return jnp.dot(a, b, preferred_element_type=F32)


def _dot_nt(a, b):
    return lax.dot_general(a, b, (((1,), (1,)), ((), ())), preferred_element_type=F32)


def _dot01_right(x, m01):
    hi, mid, lo = _split3(x)
    return _dot(hi, m01) + _dot(mid, m01) + _dot(lo, m01)


def _dot01_left(m01, x):
    hi, mid, lo = _split3(x)
    return _dot(m01, hi) + _dot(m01, mid) + _dot(m01, lo)


def _softplus(x):
    return jnp.maximum(x, 0.0) + jnp.log1p(jnp.exp(-jnp.abs(x)))


def _silu(x):
    return x * jax.nn.sigmoid(x)


def _rms(x, g):
    ms = jnp.mean(x * x, axis=-1, keepdims=True)
    return x * lax.rsqrt(ms + EPS) * g


def _norm_matmul_kernel(x_ref, g_ref, w_ref, o_ref, h_ref):
    @pl.when(pl.program_id(1) == 0)
    def _():
        h_ref[...] = _rms(x_ref[...], g_ref[...]).astype(BF16)

    o_ref[...] = _dot(h_ref[...], w_ref[...])


def norm_matmul(x, g, w, tm=1024, tn=512):
    m, k = x.shape
    n = w.shape[1]
    tm, tn = min(tm, m), min(tn, n)
    return pl.pallas_call(
        _norm_matmul_kernel,
        grid=(m // tm, n // tn),
        in_specs=[pl.BlockSpec((tm, k), lambda i, j: (i, 0)),
                  pl.BlockSpec((1, k), lambda i, j: (0, 0)),
                  pl.BlockSpec((k, tn), lambda i, j: (0, j))],
        out_specs=pl.BlockSpec((tm, tn), lambda i, j: (i, j)),
        out_shape=jax.ShapeDtypeStruct((m, n), F32),
        scratch_shapes=[pltpu.VMEM((tm, k), BF16)],
        compiler_params=_params(("parallel", "arbitrary")),
        name="norm_matmul",
    )(x, g.reshape(1, k), w)


def _out_proj_kernel(a_ref, b_ref, wa_ref, wb_ref, r_ref, o_ref):
    o_ref[...] = r_ref[...] + (_dot(a_ref[...], wa_ref[...]) + _dot(b_ref[...], wb_ref[...]))


def out_proj(a, b, w, res, tm=1024, tn=512):
    m, kh = a.shape
    n = w.shape[1]
    tm, tn = min(tm, m), min(tn, n)
    return pl.pallas_call(
        _out_proj_kernel,
        grid=(m // tm, n // tn),
        in_specs=[pl.BlockSpec((tm, kh), lambda i, j: (i, 0)),
                  pl.BlockSpec((tm, kh), lambda i, j: (i, 0)),
                  pl.BlockSpec((kh, tn), lambda i, j: (0, j)),
                  pl.BlockSpec((kh, tn), lambda i, j: (1, j)),
                  pl.BlockSpec((tm, tn), lambda i, j: (i, j))],
        out_specs=pl.BlockSpec((tm, tn), lambda i, j: (i, j)),
        out_shape=jax.ShapeDtypeStruct((m, n), F32),
        compiler_params=_params(("parallel", "arbitrary")),
        name="out_proj",
    )(a, b, w, w, res)


def _ffn_kernel(x_ref, g_ref, w1_ref, w2_ref, o_ref, h_ref):
    @pl.when(pl.program_id(1) == 0)
    def _():
        x = x_ref[...]
        h_ref[...] = _rms(x, g_ref[...]).astype(BF16)
        o_ref[...] = x

    a = jnp.maximum(_dot(h_ref[...], w1_ref[...]), 0.0)
    o_ref[...] += _dot((a * a).astype(BF16), w2_ref[...])


def ffn(x, g, w1, w2, tm=1024, tf=512):
    m, d = x.shape
    f = w1.shape[1]
    tm, tf = min(tm, m), min(tf, f)
    return pl.pallas_call(
        _ffn_kernel,
        grid=(m // tm, f // tf),
        in_specs=[pl.BlockSpec((tm, d), lambda i, j: (i, 0)),
                  pl.BlockSpec((1, d), lambda i, j: (0, 0)),
                  pl.BlockSpec((d, tf), lambda i, j: (0, j)),
                  pl.BlockSpec((tf, d), lambda i, j: (j, 0))],
        out_specs=pl.BlockSpec((tm, d), lambda i, j: (i, 0)),
        out_shape=jax.ShapeDtypeStruct((m, d), F32),
        scratch_shapes=[pltpu.VMEM((tm, d), BF16)],
        compiler_params=_params(("parallel", "arbitrary")),
        name="ffn",
    )(x, g.reshape(1, d), w1, w2)


def _head_expand():
    h = lax.broadcasted_iota(jnp.int32, (LANES, SSD_D_INNER), 0)
    c = lax.broadcasted_iota(jnp.int32, (LANES, SSD_D_INNER), 1)
    return (c // SSD_HEAD_DIM == h).astype(BF16)


def _conv_silu(pad_ref, w_ref, b_ref, lead, rows):
    acc = b_ref[...] + w_ref[SSD_CONV - 1:SSD_CONV, :] * pad_ref[lead:lead + rows, :]
    for k in range(SSD_CONV - 1):
        s = lead - (SSD_CONV - 1) + k
        acc = acc + w_ref[k:k + 1, :] * pad_ref[s:s + rows, :]
    return _silu(acc)


def _ssd_intra(xs, bc, z_unused, dt_raw, dtb, alog, e01, same_block):
    r = xs.shape[0]
    row = lax.broadcasted_iota(jnp.int32, (r, r), 0)
    col = lax.broadcasted_iota(jnp.int32, (r, r), 1)
    causal = jnp.logical_and(col <= row, same_block)
    dt = _softplus(dt_raw + dtb)
    dta = dt * (-jnp.exp(alog))
    at = _dot01_left(causal.astype(BF16), dta)
    at_t = at.T
    at_x = _dot01_right(at, e01)
    dt_x = _dot01_right(dt, e01)
    xdt = xs * dt_x
    return dict(causal=causal, at=at, at_t=at_t, at_x=at_x, xdt=xdt)


def _ssd_diag_pair(q, pr, cb, at, at_t, causal, xdt_bf):
    lane = lax.broadcasted_iota(jnp.int32, (q, LANES), 1)
    ys = []
    for h in (2 * pr, 2 * pr + 1):
        d = at[:, h:h + 1] - at_t[h:h + 1, :]
        dec = jnp.exp(jnp.where(causal, d, -jnp.inf))
        ys.append(_dot((cb * dec).astype(BF16), xdt_bf))
    return jnp.where(lane < SSD_HEAD_DIM, ys[0], ys[1])


def _gated_group_norm(y, z, g_ref, o_ref):
    gw = SSD_D_INNER // SSD_GROUPS
    for g in range(SSD_GROUPS):
        cs = slice(g * gw, (g + 1) * gw)
        yg = y[:, cs] * _silu(z[:, cs])
        o_ref[:, cs] = _rms(yg, g_ref[:, cs]).astype(o_ref.dtype)


def _ssd_prompt_kernel(z_ref, xs_ref, bc_ref, dt_ref, wcx_ref, wcb_ref, bcx_ref, bcb_ref, dtb_ref,
                       alog_ref, dsk_ref, gss_ref, e_ref, y_ref, s_ref, padx, padb, st, ybuf):
    c = pl.program_id(1)
    q = ROW_TILE
    lead = 8
    gn = SSD_GROUPS * SSD_STATE

    @pl.when(c == 0)
    def _():
        padx[0:lead, :] = jnp.zeros((lead, SSD_D_INNER), F32)
        padb[0:lead, :] = jnp.zeros((lead, 2 * gn), F32)
        st[...] = jnp.zeros(st.shape, F32)

    padx[lead:lead + q, :] = xs_ref[...]
    padb[lead:lead + q, :] = bc_ref[...]
    xs = _conv_silu(padx, wcx_ref, bcx_ref, lead, q)
    bc = _conv_silu(padb, wcb_ref, bcb_ref, lead, q)
    padx[0:lead, :] = padx[q:q + lead, :]
    padb[0:lead, :] = padb[q:q + lead, :]

    e01 = e_ref[...]
    whole = jnp.full((q, q), True)
    r = _ssd_intra(xs, bc, None, dt_ref[...], dtb_ref[...], alog_ref[...], e01, whole)
    at, at_t, at_x, xdt, causal = r["at"], r["at_t"], r["at_x"], r["xdt"], r["causal"]
    eat = jnp.exp(at_x)
    last = at_x[q - 1:q, :]
    xw = (xdt * jnp.exp(last - at_x)).astype(BF16)
    cd = jnp.exp(last)
    xdt_bf = xdt.astype(BF16)

    for g in range(SSD_GROUPS):
        bm = bc[:, g * SSD_STATE:(g + 1) * SSD_STATE]
        cm = bc[:, gn + g * SSD_STATE:gn + (g + 1) * SSD_STATE].astype(BF16)
        cb = _dot_nt(cm, bm.astype(BF16))
        bm_t = bm.T.astype(BF16)
        for pr in (2 * g, 2 * g + 1):
            cs = slice(pr * LANES, (pr + 1) * LANES)
            yd = _ssd_diag_pair(q, pr, cb, at, at_t, causal, xdt_bf[:, cs])
            s_in = st[:, cs]
            yo = _dot(cm, s_in.astype(BF16)) * eat[:, cs]
            st[:, cs] = s_in * cd[:, cs] + _dot(bm_t, xw[:, cs])
            ybuf[:, cs] = yd + yo

    y = ybuf[...] + dsk_ref[...] * xs
    _gated_group_norm(y, z_ref[...], gss_ref, y_ref)

    @pl.when(c == pl.num_programs(1) - 1)
    def _():
        for pr in range(SSD_D_INNER // LANES):
            cs = slice(pr * LANES, (pr + 1) * LANES)
            s_ref[0, cs, :] = st[:, cs].T


def ssd_prompt(proj, dtp, batch, seq, w_conv, b_conv, dt_bias, a_log, d_skip, g_ssd):
    nc = seq // ROW_TILE
    gn = SSD_GROUPS * SSD_STATE
    d = SSD_D_INNER
    row = lambda b, c: b * nc + c
    const = lambda shape: pl.BlockSpec(shape, lambda b, c: (0, 0))
    pad128 = lambda v: jnp.pad(v, (0, LANES - v.shape[0])).reshape(1, LANES)
    return pl.pallas_call(
        _ssd_prompt_kernel,
        grid=(batch, nc),
        in_specs=[pl.BlockSpec((ROW_TILE, d), lambda b, c: (row(b, c), 0)),
                  pl.BlockSpec((ROW_TILE, d), lambda b, c: (row(b, c), 1)),
                  pl.BlockSpec((ROW_TILE, 2 * gn), lambda b, c: (row(b, c), 2)),
                  pl.BlockSpec((ROW_TILE, LANES), lambda b, c: (row(b, c), 0)),
                  const((SSD_CONV, d)), const((SSD_CONV, 2 * gn)), const((1, d)), const((1, 2 * gn)),
                  const((1, LANES)), const((1, LANES)), const((1, d)), const((1, d)),
                  const((LANES, d))],
        out_specs=[pl.BlockSpec((ROW_TILE, d), lambda b, c: (row(b, c), 0)),
                   pl.BlockSpec((1, d, SSD_STATE), lambda b, c: (b, 0, 0))],
        out_shape=[jax.ShapeDtypeStruct((batch * seq, d), BF16),
                   jax.ShapeDtypeStruct((batch, d, SSD_STATE), F32)],
        scratch_shapes=[pltpu.VMEM((ROW_TILE + 8, d), F32), pltpu.VMEM((ROW_TILE + 8, 2 * gn), F32),
                        pltpu.VMEM((SSD_STATE, d), F32), pltpu.VMEM((ROW_TILE, d), F32)],
        compiler_params=_params(("parallel", "arbitrary")),
        name="ssd_prompt",
    )(proj, proj, proj, dtp, w_conv[:, :d], w_conv[:, d:], b_conv[:d].reshape(1, d),
      b_conv[d:].reshape(1, 2 * gn), pad128(dt_bias), pad128(a_log),
      jnp.repeat(d_skip, SSD_HEAD_DIM).reshape(1, d), g_ssd.reshape(1, d), _head_expand())


def _ssd_sample_kernel(z_ref, xs_ref, bc_ref, dt_ref, cvx_ref, cvb_ref, s0_ref, wcx_ref, wcb_ref,
                       bcx_ref, bcb_ref, dtb_ref, alog_ref, dsk_ref, gss_ref, e_ref, et_ref,
                       y_ref, cox_ref, cob_ref, s_ref,
                       padx, padb, ybuf, yoff, eat_s, cm_s, bm_s, xwt_s, att_s):
    j = pl.program_id(1)
    r = ROW_TILE
    t = xs_ref.shape[0] // cvx_ref.shape[0]
    nseq = r // t
    lead = 8
    gn = SSD_GROUPS * SSD_STATE
    d = SSD_D_INNER
    hist = SSD_CONV - 1

    @pl.when(j == 0)
    def _():
        def conv(pad, raw_ref, cv_ref, w_ref, b_ref, co_ref):
            width = raw_ref.shape[1]
            pad[:, lead - hist:lead, :] = cv_ref[...]
            pad[:, lead:lead + t, :] = raw_ref[...].reshape(nseq, t, width)
            acc = b_ref[...] + w_ref[hist:hist + 1, :] * pad[:, lead:lead + t, :]
            for k in range(hist):
                s = lead - hist + k
                acc = acc + w_ref[k:k + 1, :] * pad[:, s:s + t, :]
            co_ref[...] = pad[:, lead + t - hist:lead + t, :]
            return _silu(acc).reshape(r, width)

        xs = conv(padx, xs_ref, cvx_ref, wcx_ref, bcx_ref, cox_ref)
        bc = conv(padb, bc_ref, cvb_ref, wcb_ref, bcb_ref, cob_ref)
        e01 = e_ref[...]
        row = lax.broadcasted_iota(jnp.int32, (r, r), 0)
        col = lax.broadcasted_iota(jnp.int32, (r, r), 1)
        same = (row // t) == (col // t)
        q = _ssd_intra(xs, bc, None, dt_ref[...], dtb_ref[...], alog_ref[...], e01, same)
        at, at_t, at_x, xdt, causal = q["at"], q["at_t"], q["at_x"], q["xdt"], q["causal"]
        eat_s[...] = jnp.exp(at_x)
        last_sel = jnp.logical_and(same, (col % t) == t - 1).astype(BF16)
        last = _dot01_left(last_sel, at_x)
        xw = xdt * jnp.exp(last - at_x)
        xdt_bf = xdt.astype(BF16)
        att_s[...] = _dot01_left(et_ref[...], at_t)
        for g in range(SSD_GROUPS):
            bm = bc[:, g * SSD_STATE:(g + 1) * SSD_STATE].astype(BF16)
            cm = bc[:, gn + g * SSD_STATE:gn + (g + 1) * SSD_STATE].astype(BF16)
            bm_s[:, g * SSD_STATE:(g + 1) * SSD_STATE] = bm
            cm_s[:, g * SSD_STATE:(g + 1) * SSD_STATE] = cm
            cb = _dot_nt(cm, bm)
            for pr in (2 * g, 2 * g + 1):
                cs = slice(pr * LANES, (pr + 1) * LANES)
                ybuf[:, cs] = _ssd_diag_pair(r, pr, cb, at, at_t, causal, xdt_bf[:, cs])
                xwt_s[cs, :] = xw[:, cs].T.astype(BF16)
        ybuf[...] = ybuf[...] + dsk_ref[...] * xs
        yoff[...] = jnp.zeros(yoff.shape, F32)

    rowseq = lax.broadcasted_iota(jnp.int32, (r, LANES), 0) // t
    laneseq = lax.broadcasted_iota(jnp.int32, (LANES, r), 1) // t
    lastlane = lax.broadcasted_iota(jnp.int32, (LANES, r), 1) == j * t + t - 1
    gw = d // SSD_GROUPS
    for g in range(SSD_GROUPS):
        rs = slice(g * gw, (g + 1) * gw)
        gs = slice(g * SSD_STATE, (g + 1) * SSD_STATE)
        s0 = s0_ref[0, rs, :]
        yo = _dot_nt(cm_s[:, gs], s0.astype(BF16))
        for half in range(gw // LANES):
            cs = slice(g * gw + half * LANES, g * gw + (half + 1) * LANES)
            yoff[:, cs] += jnp.where(rowseq == j, yo[:, half * LANES:(half + 1) * LANES], 0.0)
        sel = jnp.concatenate([laneseq, laneseq], axis=0) == j
        sc = _dot(jnp.where(sel, xwt_s[rs, :], jnp.zeros((), BF16)), bm_s[:, gs])
        lastc = jnp.concatenate([lastlane, lastlane], axis=0)
        cdc = jnp.exp(jnp.sum(jnp.where(lastc, att_s[rs, :], 0.0), axis=1, keepdims=True))
        s_ref[0, rs, :] = s0 * cdc + sc

    @pl.when(j == pl.num_programs(1) - 1)
    def _():
        y = ybuf[...] + yoff[...] * eat_s[...]
        _gated_group_norm(y, z_ref[...], gss_ref, y_ref)


def ssd_sample(proj, dtp, nseq, t, state_conv, state_ssm, w_conv, b_conv, dt_bias, a_log, d_skip, g_ssd):
    spb = ROW_TILE // t
    nb = nseq // spb
    gn = SSD_GROUPS * SSD_STATE
    d = SSD_D_INNER
    hist = SSD_CONV - 1
    const = lambda shape: pl.BlockSpec(shape, lambda i, j: (0, 0))
    pad128 = lambda v: jnp.pad(v, (0, LANES - v.shape[0])).reshape(1, LANES)
    e01 = _head_expand()
    outs = pl.pallas_call(
        _ssd_sample_kernel,
        grid=(nb, spb),
        in_specs=[pl.BlockSpec((ROW_TILE, d), lambda i, j: (i, 0)),
                  pl.BlockSpec((ROW_TILE, d), lambda i, j: (i, 1)),
                  pl.BlockSpec((ROW_TILE, 2 * gn), lambda i, j: (i, 2)),
                  pl.BlockSpec((ROW_TILE, LANES), lambda i, j: (i, 0)),
                  pl.BlockSpec((spb, hist, d), lambda i, j: (i, 0, 0)),
                  pl.BlockSpec((spb, hist, 2 * gn), lambda i, j: (i, 0, 1)),
                  pl.BlockSpec((1, d, SSD_STATE), lambda i, j: (i * spb + j, 0, 0)),
                  const((SSD_CONV, d)), const((SSD_CONV, 2 * gn)), const((1, d)), const((1, 2 * gn)),
                  const((1, LANES)), const((1, LANES)), const((1, d)), const((1, d)),
                  const((LANES, d)), const((d, LANES))],
        out_specs=[pl.BlockSpec((ROW_TILE, d), lambda i, j: (i, 0)),
                   pl.BlockSpec((spb, hist, d), lambda i, j: (i, 0, 0)),
                   pl.BlockSpec((spb, hist, 2 * gn), lambda i, j: (i, 0, 0)),
                   pl.BlockSpec((1, d, SSD_STATE), lambda i, j: (i * spb + j, 0, 0))],
        out_shape=[jax.ShapeDtypeStruct((nseq * t, d), BF16),
                   jax.ShapeDtypeStruct((nseq, hist, d), F32),
                   jax.ShapeDtypeStruct((nseq, hist, 2 * gn), F32),
                   jax.ShapeDtypeStruct((nseq, d, SSD_STATE), F32)],
        scratch_shapes=[pltpu.VMEM((spb, 8 + t, d), F32), pltpu.VMEM((spb, 8 + t, 2 * gn), F32),
                        pltpu.VMEM((ROW_TILE, d), F32), pltpu.VMEM((ROW_TILE, d), F32),
                        pltpu.VMEM((ROW_TILE, d), F32),
                        pltpu.VMEM((ROW_TILE, gn), BF16), pltpu.VMEM((ROW_TILE, gn), BF16),
                        pltpu.VMEM((d, ROW_TILE), BF16), pltpu.VMEM((d, ROW_TILE), F32)],
        compiler_params=_params(("parallel", "arbitrary")),
        name="ssd_sample",
    )(proj, proj, proj, dtp, state_conv, state_conv, state_ssm,
      w_conv[:, :d], w_conv[:, d:], b_conv[:d].reshape(1, d), b_conv[d:].reshape(1, 2 * gn),
      pad128(dt_bias), pad128(a_log), jnp.repeat(d_skip, SSD_HEAD_DIM).reshape(1, d),
      g_ssd.reshape(1, d), e01, e01.T)
    y, cox, cob, s_new = outs
    return y, jnp.concatenate([cox, cob], axis=-1), s_new


def _later_matrix():
    j = lax.broadcasted_iota(jnp.int32, (LANES, 2 * LANES), 0)
    s = lax.broadcasted_iota(jnp.int32, (LANES, 2 * LANES), 1)
    return jnp.logical_or(j > s, s >= LANES).astype(BF16)


def _sb_block(z, vis, carry, later01):
    sp = _softplus(z)
    lf = -sp
    ls = z - sp
    if vis is not None:
        lf = jnp.where(vis, lf, 0.0)
    hi = lf.astype(BF16)
    lo = (lf - hi.astype(F32)).astype(BF16)
    lc = _dot(hi, later01) + _dot(lo, later01)
    w = jnp.exp(ls + lc[:, :LANES] + carry)
    if vis is not None:
        w = jnp.where(vis, w, 0.0)
    return w, carry + lc[:, LANES:]


def _sb_prompt_kernel(q_ref, k_ref, v_ref, o_ref, acc, carry):
    qi = pl.program_id(2)
    t = ROW_TILE
    g4 = SB_Q_PER_KV
    later01 = _later_matrix()
    q = q_ref[...]
    qs = jnp.concatenate([q[:, g * SB_HEAD_DIM:(g + 1) * SB_HEAD_DIM] for g in range(g4)], axis=0)
    qs = (qs * SB_SCALE).astype(BF16)
    rows = g4 * t

    def step(j, vis):
        ks = k_ref[pl.ds(pl.multiple_of(j * t, t), t), :].astype(BF16)
        vs = v_ref[pl.ds(pl.multiple_of(j * t, t), t), :].astype(BF16)
        w, c = _sb_block(_dot_nt(qs, ks), vis, carry[...], later01)
        carry[...] = c
        acc[...] += _dot(w.astype(BF16), vs)

    acc[...] = jnp.zeros(acc.shape, F32)
    carry[...] = jnp.zeros(carry.shape, F32)
    row_t = lax.broadcasted_iota(jnp.int32, (rows, t), 0) % t
    col = lax.broadcasted_iota(jnp.int32, (rows, t), 1)
    step(qi, col < row_t)

    def body(i, _):
        step(qi - 1 - i, None)
        return 0

    lax.fori_loop(0, qi, body, 0)
    for g in range(g4):
        o_ref[:, g * SB_HEAD_DIM:(g + 1) * SB_HEAD_DIM] = acc[g * t:(g + 1) * t, :].astype(o_ref.dtype)


def sb_prompt(proj, batch, seq):
    nq = seq // ROW_TILE
    qw = SB_Q_PER_KV * SB_HEAD_DIM
    q0 = (A_XBC) // qw
    k0 = (A_XBC + SB_HEADS * SB_HEAD_DIM) // SB_HEAD_DIM
    v0 = k0 + SB_KV_HEADS
    return pl.pallas_call(
        _sb_prompt_kernel,
        grid=(batch, SB_KV_HEADS, nq),
        in_specs=[pl.BlockSpec((ROW_TILE, qw), lambda b, h, i: (b * nq + i, q0 + h)),
                  pl.BlockSpec((seq, SB_HEAD_DIM), lambda b, h, i: (b, k0 + h)),
                  pl.BlockSpec((seq, SB_HEAD_DIM), lambda b, h, i: (b, v0 + h))],
        out_specs=pl.BlockSpec((ROW_TILE, qw), lambda b, h, i: (b * nq + i, h)),
        out_shape=jax.ShapeDtypeStruct((batch * seq, SB_HEADS * SB_HEAD_DIM), BF16),
        scratch_shapes=[pltpu.VMEM((SB_Q_PER_KV * ROW_TILE, SB_HEAD_DIM), F32),
                        pltpu.VMEM((SB_Q_PER_KV * ROW_TILE, LANES), F32)],
        compiler_params=_params(("parallel", "parallel", "arbitrary")),
        name="sb_prompt",
    )(proj, proj, proj)


def _sb_sample_kernel(pt_ref, q_ref, kn_ref, vn_ref, *refs, pages_per_step, t):
    k_refs = refs[:pages_per_step]
    v_refs = refs[pages_per_step:2 * pages_per_step]
    o_ref, acc, carry, knp, vnp = refs[2 * pages_per_step:]
    p = pl.program_id(1)
    rows = q_ref.shape[1]
    rk = rows // SB_KV_HEADS
    later01 = _later_matrix()
    qs = (q_ref[0] * SB_SCALE).astype(BF16)

    def block(kfn, vfn, vis):
        z = jnp.concatenate([_dot_nt(qs[h * rk:(h + 1) * rk], kfn(h)) for h in range(SB_KV_HEADS)], axis=0)
        w, c = _sb_block(z, vis, carry[...], later01)
        carry[...] = c
        wb = w.astype(BF16)
        acc[...] += jnp.concatenate([_dot(wb[h * rk:(h + 1) * rk], vfn(h)) for h in range(SB_KV_HEADS)], axis=0)

    @pl.when(p == 0)
    def _():
        acc[...] = jnp.zeros(acc.shape, F32)
        carry[...] = jnp.zeros(carry.shape, F32)
        knp[...] = jnp.zeros(knp.shape, F32)
        vnp[...] = jnp.zeros(vnp.shape, F32)
        knp[0:t, :] = kn_ref[...]
        vnp[0:t, :] = vn_ref[...]
        row_t = lax.broadcasted_iota(jnp.int32, (rows, LANES), 0) % t
        col = lax.broadcasted_iota(jnp.int32, (rows, LANES), 1)
        hd = lambda ref: (lambda h: ref[:, h * SB_HEAD_DIM:(h + 1) * SB_HEAD_DIM].astype(BF16))
        block(hd(knp), hd(vnp), col < row_t)

    for i in range(pages_per_step):
        block(lambda h: k_refs[i][:, h, :].astype(BF16), lambda h: v_refs[i][:, h, :].astype(BF16), None)

    @pl.when(p == pl.num_programs(1) - 1)
    def _():
        o_ref[0] = acc[...].astype(o_ref.dtype)


def sb_sample(q_rows, proj, cache_k, cache_v, layer, page_table, t, pages_per_step=8):
    nseq, n_pages = page_table.shape
    pps = min(pages_per_step, n_pages)
    nsteps = n_pages // pps
    kw = SB_KV_HEADS * SB_HEAD_DIM
    k0 = (A_XBC + SB_HEADS * SB_HEAD_DIM) // kw
    rows = q_rows.shape[1]

    def page_spec(i):
        return pl.BlockSpec((None, None, PAGE_SIZE, SB_KV_HEADS, SB_HEAD_DIM),
                            lambda b, p, pt: (layer, pt[b, n_pages - 1 - (p * pps + i)], 0, 0, 0))

    grid_spec = pltpu.PrefetchScalarGridSpec(
        num_scalar_prefetch=1,
        grid=(nseq, nsteps),
        in_specs=[pl.BlockSpec((1, rows, SB_HEAD_DIM), lambda b, p, pt: (b, 0, 0)),
                  pl.BlockSpec((t, kw), lambda b, p, pt: (b, k0)),
                  pl.BlockSpec((t, kw), lambda b, p, pt: (b, k0 + 1))]
                 + [page_spec(i) for i in range(pps)] + [page_spec(i) for i in range(pps)],
        out_specs=pl.BlockSpec((1, rows, SB_HEAD_DIM), lambda b, p, pt: (b, 0, 0)),
        scratch_shapes=[pltpu.VMEM((rows, SB_HEAD_DIM), F32), pltpu.VMEM((rows, LANES), F32),
                        pltpu.VMEM((PAGE_SIZE, kw), F32), pltpu.VMEM((PAGE_SIZE, kw), F32)],
    )
    return pl.pallas_call(
        functools.partial(_sb_sample_kernel, pages_per_step=pps, t=t),
        grid_spec=grid_spec,
        out_shape=jax.ShapeDtypeStruct((nseq, rows, SB_HEAD_DIM), BF16),
        compiler_params=_params(("parallel", "arbitrary")),
        name="sb_sample",
    )(page_table, q_rows, proj, proj, *([cache_k] * pps), *([cache_v] * pps))


def _layer_a(xp, xs, batch, seq, nseq, t, cache_k, cache_v, layer, state_ssm, state_conv, page_table,
             g_mix, g_ffn, w_in, w_conv, b_conv, dt_bias, a_log, d_skip, g_ssd, w_out, w1, w2):
    w_main = jnp.concatenate([w_in[:, :A_XBC], w_in[:, A_DT:]], axis=1).astype(BF16)
    w_dt = jnp.pad(w_in[:, A_XBC:A_DT], ((0, 0), (0, LANES - SSD_HEADS))).astype(BF16)
    w_out = w_out.astype(BF16)
    w1 = w1.astype(BF16)
    w2 = w2.astype(BF16)
    gn2 = 2 * SSD_GROUPS * SSD_STATE
    qcol = A_XBC
    kcol = qcol + SB_HEADS * SB_HEAD_DIM
    vcol = kcol + SB_KV_HEADS * SB_HEAD_DIM

    pp = norm_matmul(xp, g_mix, w_main)
    dp = norm_matmul(xp, g_mix, w_dt)
    y_p, ssm_p = ssd_prompt(pp, dp, batch, seq, w_conv, b_conv, dt_bias, a_log, d_skip, g_ssd)
    o_p = sb_prompt(pp, batch, seq)
    xp = out_proj(y_p, o_p, w_out, xp)
    xp = ffn(xp, g_ffn, w1, w2)
    sbk_p = pp[:, kcol:vcol].reshape(batch, seq, SB_KV_HEADS, SB_HEAD_DIM)
    sbv_p = pp[:, vcol:].reshape(batch, seq, SB_KV_HEADS, SB_HEAD_DIM)
    conv_p = pp.reshape(batch, seq, -1)[:, seq - (SSD_CONV - 1):, A_Z:A_Z + SSD_CONV_DIM]
    ssm_p = ssm_p.reshape(batch, SSD_HEADS, SSD_HEAD_DIM, SSD_STATE)

    ps = norm_matmul(xs, g_mix, w_main)
    ds = norm_matmul(xs, g_mix, w_dt)
    y_s, conv_s, ssm_s = ssd_sample(ps, ds, nseq, t, state_conv, state_ssm.reshape(nseq, SSD_D_INNER, SSD_STATE),
                                    w_conv, b_conv, dt_bias, a_log, d_skip, g_ssd)
    q_rows = ps[:, qcol:kcol].reshape(nseq, t, SB_KV_HEADS, SB_Q_PER_KV, SB_HEAD_DIM)
    q_rows = q_rows.transpose(0, 2, 3, 1, 4).reshape(nseq, SB_HEADS * t, SB_HEAD_DIM)
    o_s = sb_sample(q_rows, ps, cache_k, cache_v, layer, page_table, t)
    o_s = o_s.reshape(nseq, SB_KV_HEADS, SB_Q_PER_KV, t, SB_HEAD_DIM).transpose(0, 3, 1, 2, 4)
    o_s = o_s.reshape(nseq * t, SB_HEADS * SB_HEAD_DIM)
    xs = out_proj(y_s, o_s, w_out, xs)
    xs = ffn(xs, g_ffn, w1, w2)
    sbk_s = ps[:, kcol:vcol].reshape(nseq, t, SB_KV_HEADS, SB_HEAD_DIM)
    sbv_s = ps[:, vcol:].reshape(nseq, t, SB_KV_HEADS, SB_HEAD_DIM)
    ssm_s = ssm_s.reshape(nseq, SSD_HEADS, SSD_HEAD_DIM, SSD_STATE)
    return xp, xs, (sbk_p, sbv_p, ssm_p, conv_p), (sbk_s, sbv_s, ssm_s, conv_s)


MLA_PREP_ROWS = 256
C_MAIN = 2 * CMLP_DIM + MLA_Q_RANK + MLA_KV_RANK


def _half_head_ones():
    a = lax.broadcasted_iota(jnp.int32, (LANES, LANES), 0) // MLA_ROPE
    b = lax.broadcasted_iota(jnp.int32, (LANES, LANES), 1) // MLA_ROPE
    return (a == b).astype(BF16)


def _rope_norm(x, g, cos, sin, ones64):
    ms = _dot01_right(x * x, ones64) * (1.0 / MLA_ROPE)
    xn = x * lax.rsqrt(ms + EPS) * g
    lane = lax.broadcasted_iota(jnp.int32, x.shape, 1) % MLA_ROPE
    rot = jnp.where(lane < MLA_ROPE // 2, -pltpu.roll(xn, LANES - MLA_ROPE // 2, 1), pltpu.roll(xn, MLA_ROPE // 2, 1))
    return xn * cos + rot * sin


def _mla_prep_kernel(cq_ref, ckv_ref, kpe_ref, cos_ref, sin_ref, gcq_ref, gckv_ref, gqn_ref, gqr_ref, gkr_ref,
                     wqn_ref, wqr_ref, *rest, with_kv):
    if with_kv:
        gkn_ref, wuk_ref, wuv_ref, qn_ref, qp_ref, ckvn_ref, kper_ref, kn_ref, vv_ref, kp2_ref = rest
    else:
        qn_ref, qp_ref, ckvn_ref, kper_ref = rest
    ones64 = _half_head_ones()
    cos, sin = cos_ref[...], sin_ref[...]
    cqn = _rms(cq_ref[...], gcq_ref[...]).astype(BF16)
    qn = _dot(cqn, wqn_ref[...])
    qp = _dot(cqn, wqr_ref[...])
    for h in range(MLA_HEADS):
        cs = slice(h * MLA_NOPE, (h + 1) * MLA_NOPE)
        qn_ref[:, cs] = (_rms(qn[:, cs], gqn_ref[:, cs]) * MLA_SCALE).astype(qn_ref.dtype)
    for b in range(MLA_HEADS * MLA_ROPE // LANES):
        cs = slice(b * LANES, (b + 1) * LANES)
        qp_ref[:, cs] = (_rope_norm(qp[:, cs], gqr_ref[...], cos, sin, ones64) * MLA_SCALE).astype(qp_ref.dtype)
    ckvn = _rms(ckv_ref[...], gckv_ref[...])
    ckvn_ref[...] = ckvn
    kper = _rope_norm(kpe_ref[...], gkr_ref[...], cos, sin, ones64)
    kper_ref[...] = kper
    if with_kv:
        cb = ckvn.astype(BF16)
        kraw = _dot(cb, wuk_ref[...])
        for h in range(MLA_HEADS):
            cs = slice(h * MLA_NOPE, (h + 1) * MLA_NOPE)
            kn_ref[:, cs] = _rms(kraw[:, cs], gkn_ref[...]).astype(kn_ref.dtype)
        vv_ref[...] = _dot(cb, wuv_ref[...]).astype(vv_ref.dtype)
        kp2_ref[...] = (kper + pltpu.roll(kper, MLA_ROPE, 1)).astype(kp2_ref.dtype)


def mla_prep(proj, kpe_raw, cos, sin, g_cq, g_ckv, g_qn_row, g_qr, g_kr, wq_nope, wq_rope, kv_weights=None):
    m = proj.shape[0]
    tr = min(MLA_PREP_ROWS, m)
    ncos = cos.shape[0] // tr
    c0 = 2 * CMLP_DIM // MLA_Q_RANK
    hq = MLA_HEADS * MLA_NOPE
    hr = MLA_HEADS * MLA_ROPE
    with_kv = kv_weights is not None
    const = lambda shape: pl.BlockSpec(shape, lambda i: (0, 0))
    row = lambda w: pl.BlockSpec((tr, w), lambda i: (i, 0))
    pad_row = lambda v: jnp.concatenate([v, jnp.zeros((LANES - v.shape[0],), F32)]).reshape(1, LANES)
    in_specs = [pl.BlockSpec((tr, MLA_Q_RANK), lambda i: (i, c0)),
                pl.BlockSpec((tr, MLA_KV_RANK), lambda i: (i, c0 + 1)),
                row(LANES),
                pl.BlockSpec((tr, LANES), lambda i: (i % ncos, 0)),
                pl.BlockSpec((tr, LANES), lambda i: (i % ncos, 0)),
                const((1, MLA_Q_RANK)), const((1, MLA_KV_RANK)), const((1, hq)), const((1, LANES)), const((1, LANES)),
                const((MLA_Q_RANK, hq)), const((MLA_Q_RANK, hr))]
    args = [proj, proj, kpe_raw, cos, sin, g_cq.reshape(1, -1), g_ckv.reshape(1, -1), g_qn_row.reshape(1, hq),
            jnp.tile(g_qr, LANES // MLA_ROPE).reshape(1, LANES), pad_row(g_kr), wq_nope, wq_rope]
    out_specs = [row(hq), row(hr), row(MLA_KV_RANK), row(LANES)]
    out_shape = [jax.ShapeDtypeStruct((m, hq), BF16), jax.ShapeDtypeStruct((m, hr), BF16),
                 jax.ShapeDtypeStruct((m, MLA_KV_RANK), F32), jax.ShapeDtypeStruct((m, LANES), F32)]
    if with_kv:
        g_kn, w_uk, w_uv = kv_weights
        in_specs += [const((1, MLA_NOPE)), const((MLA_KV_RANK, hq)), const((MLA_KV_RANK, MLA_HEADS * MLA_V))]
        args += [g_kn.reshape(1, -1), w_uk, w_uv]
        out_specs += [row(hq), row(MLA_HEADS * MLA_V), row(LANES)]
        out_shape += [jax.ShapeDtypeStruct((m, hq), BF16), jax.ShapeDtypeStruct((m, MLA_HEADS * MLA_V), BF16),
                      jax.ShapeDtypeStruct((m, LANES), BF16)]
    return pl.pallas_call(
        functools.partial(_mla_prep_kernel, with_kv=with_kv),
        grid=(m // tr,),
        in_specs=in_specs, out_specs=out_specs, out_shape=out_shape,
        compiler_params=_params(("parallel",)),
        name="mla_prep",
    )(*args)


def _softmax_update(s, v, m_ref, l_ref, acc_ref):
    m_old = m_ref[...]
    m_new = jnp.maximum(m_old, jnp.max(s, axis=-1, keepdims=True))
    alpha = jnp.exp(m_old - m_new)
    p = jnp.exp(s - m_new[:, :1])
    l_ref[...] = alpha * l_ref[...] + jnp.sum(p, axis=-1, keepdims=True)
    acc_ref[...] = alpha[:, :1] * acc_ref[...] + _dot(p.astype(BF16), v)
    m_ref[...] = m_new


def _mla_prompt_kernel(qn_ref, qp_ref, kn_ref, kp_ref, vv_ref, o_ref, m_s, l_s, acc_s, *, tq):
    qi = pl.program_id(2)
    lane_head = lax.broadcasted_iota(jnp.int32, (tq, LANES), 1) // MLA_ROPE
    qp = qp_ref[...]
    row = lax.broadcasted_iota(jnp.int32, (tq, tq), 0)
    col = lax.broadcasted_iota(jnp.int32, (tq, tq), 1)
    for hh in range(2):
        cs = slice(hh * MLA_NOPE, (hh + 1) * MLA_NOPE)
        q1 = qn_ref[:, cs]
        q2 = jnp.where(lane_head == hh, qp, jnp.zeros((), qp.dtype))
        m_s[...] = jnp.full(m_s.shape, -jnp.inf, F32)
        l_s[...] = jnp.zeros(l_s.shape, F32)
        acc_s[...] = jnp.zeros(acc_s.shape, F32)

        def scores(j):
            ks = pl.ds(pl.multiple_of(j * tq, tq), tq)
            return _dot_nt(q1, kn_ref[ks, cs]) + _dot_nt(q2, kp_ref[ks, :]), vv_ref[ks, cs]

        def body(j, _):
            s, v = scores(j)
            _softmax_update(s, v, m_s, l_s, acc_s)
            return 0

        lax.fori_loop(0, qi, body, 0)
        s, v = scores(qi)
        _softmax_update(jnp.where(col <= row, s, -jnp.inf), v, m_s, l_s, acc_s)
        o_ref[:, cs] = (acc_s[...] / l_s[:, :1]).astype(o_ref.dtype)


def mla_prompt(qn, qp, kn, kp2, vv, batch, seq, tq=256):
    tq = min(tq, seq)
    nq = seq // tq
    hw = 2 * MLA_NOPE
    return pl.pallas_call(
        functools.partial(_mla_prompt_kernel, tq=tq),
        grid=(batch, MLA_HEADS // 2, nq),
        in_specs=[pl.BlockSpec((tq, hw), lambda b, h, i: (b * nq + i, h)),
                  pl.BlockSpec((tq, LANES), lambda b, h, i: (b * nq + i, h)),
                  pl.BlockSpec((seq, hw), lambda b, h, i: (b, h)),
                  pl.BlockSpec((seq, LANES), lambda b, h, i: (b, 0)),
                  pl.BlockSpec((seq, hw), lambda b, h, i: (b, h))],
        out_specs=pl.BlockSpec((tq, hw), lambda b, h, i: (b * nq + i, h)),
        out_shape=jax.ShapeDtypeStruct((batch * seq, MLA_HEADS * MLA_V), BF16),
        scratch_shapes=[pltpu.VMEM((tq, LANES), F32), pltpu.VMEM((tq, LANES), F32), pltpu.VMEM((tq, MLA_V), F32)],
        compiler_params=_params(("parallel", "parallel", "arbitrary")),
        name="mla_prompt",
    )(qn, qp, kn, kp2, vv)


def _head_matmul_kernel(a_ref, w_ref, o_ref):
    a = a_ref[...]
    a = a.reshape(-1, a.shape[-1]).astype(BF16)
    o_ref[...] = _dot(a, w_ref[0]).reshape(o_ref.shape).astype(o_ref.dtype)


def absorb_queries(qg, w_uk_t, nseq, t):
    return pl.pallas_call(
        _head_matmul_kernel,
        grid=(MLA_HEADS,),
        in_specs=[pl.BlockSpec((nseq * t, MLA_NOPE), lambda h: (0, h)),
                  pl.BlockSpec((1, MLA_NOPE, MLA_KV_RANK), lambda h: (h, 0, 0))],
        out_specs=pl.BlockSpec((nseq, None, t, MLA_KV_RANK), lambda h: (0, h, 0, 0)),
        out_shape=jax.ShapeDtypeStruct((nseq, MLA_HEADS, t, MLA_KV_RANK), F32),
        compiler_params=_params(("parallel",)),
        name="absorb_queries",
    )(qg, w_uk_t)


def project_latent_out(o_lat, w_uv_h, nseq, t):
    return pl.pallas_call(
        _head_matmul_kernel,
        grid=(MLA_HEADS,),
        in_specs=[pl.BlockSpec((nseq, None, t, MLA_KV_RANK), lambda h: (0, h, 0, 0)),
                  pl.BlockSpec((1, MLA_KV_RANK, MLA_V), lambda h: (h, 0, 0))],
        out_specs=pl.BlockSpec((nseq * t, MLA_V), lambda h: (0, h)),
        out_shape=jax.ShapeDtypeStruct((nseq * t, MLA_HEADS * MLA_V), BF16),
        compiler_params=_params(("parallel",)),
        name="project_latent_out",
    )(o_lat, w_uv_h)


def _mla_sample_kernel(pt_ref, qa_ref, qp_ref, cn_ref, kn_ref, wuk_ref, *refs, pages_per_step, t):
    c_refs = refs[:pages_per_step]
    k_refs = refs[pages_per_step:2 * pages_per_step]
    o_ref, lhs, m_s, l_s, acc_s, cpad, kpad = refs[2 * pages_per_step:]
    b = pl.program_id(0)
    p = pl.program_id(1)
    nk = MLA_HEADS * MLA_NOPE
    rows = MLA_HEADS * t

    @pl.when(jnp.logical_and(b == 0, p == 0))
    def _():
        lhs[0:nk, :] = wuk_ref[...]

    def block(c, kpe, mask):
        keys = c.shape[0]
        kt = _dot_nt(lhs[...], c)
        k2 = kt[0:nk, :]
        k2 = (k2 * k2).reshape(MLA_HEADS, MLA_NOPE // 8, 8, keys).sum(axis=1)
        for sh in (4, 2, 1):
            k2 = k2 + pltpu.roll(k2, sh, 1)
        rs = lax.rsqrt(k2 * (1.0 / MLA_NOPE) + EPS).reshape(rows, keys)
        s = kt[nk:nk + rows, :] * rs + _dot_nt(qp_ref[0].astype(BF16), kpe)
        if mask is not None:
            s = jnp.where(mask, s, -jnp.inf)
        _softmax_update(s, c, m_s, l_s, acc_s)

    @pl.when(p == 0)
    def _():
        lhs[nk:nk + rows, :] = qa_ref[0].reshape(rows, MLA_KV_RANK).astype(BF16)
        m_s[...] = jnp.full(m_s.shape, -jnp.inf, F32)
        l_s[...] = jnp.zeros(l_s.shape, F32)
        acc_s[...] = jnp.zeros(acc_s.shape, F32)
        cpad[...] = jnp.zeros(cpad.shape, F32)
        kpad[...] = jnp.zeros(kpad.shape, F32)
        cpad[0:t, :] = cn_ref[...]
        kpad[0:t, :] = kn_ref[...]
        row_t = lax.broadcasted_iota(jnp.int32, (rows, LANES), 0) % t
        col = lax.broadcasted_iota(jnp.int32, (rows, LANES), 1)
        block(cpad[...].astype(BF16), kpad[:, 0:MLA_ROPE].astype(BF16), col <= row_t)

    for i in range(0, pages_per_step, 2):
        c = jnp.concatenate([c_refs[i][...], c_refs[i + 1][...]], axis=0).astype(BF16)
        kpe = jnp.concatenate([k_refs[i][...], k_refs[i + 1][...]], axis=0).astype(BF16)
        block(c, kpe, None)

    @pl.when(p == pl.num_programs(1) - 1)
    def _():
        o_ref[0] = (acc_s[...] / l_s[:, :1]).reshape(o_ref.shape[1:])


def mla_sample(q_abs, q_pe, ckv_new, kpe_new, w_uk_t2, cache_ckv, cache_kpe, layer, page_table, t, pages_per_step=8):
    nseq, n_pages = page_table.shape
    pps = min(pages_per_step, n_pages)
    nsteps = n_pages // pps
    rows = MLA_HEADS * t
    nk = MLA_HEADS * MLA_NOPE

    def page_spec(width, i):
        return pl.BlockSpec((None, None, PAGE_SIZE, width), lambda b, p, pt: (layer, pt[b, p * pps + i], 0, 0))

    grid_spec = pltpu.PrefetchScalarGridSpec(
        num_scalar_prefetch=1,
        grid=(nseq, nsteps),
        in_specs=[pl.BlockSpec((1, MLA_HEADS, t, MLA_KV_RANK), lambda b, p, pt: (b, 0, 0, 0)),
                  pl.BlockSpec((1, rows, MLA_ROPE), lambda b, p, pt: (b, 0, 0)),
                  pl.BlockSpec((t, MLA_KV_RANK), lambda b, p, pt: (b, 0)),
                  pl.BlockSpec((t, LANES), lambda b, p, pt: (b, 0)),
                  pl.BlockSpec((nk, MLA_KV_RANK), lambda b, p, pt: (0, 0))]
                 + [page_spec(MLA_KV_RANK, i) for i in range(pps)] + [page_spec(MLA_ROPE, i) for i in range(pps)],
        out_specs=pl.BlockSpec((1, MLA_HEADS, t, MLA_KV_RANK), lambda b, p, pt: (b, 0, 0, 0)),
        scratch_shapes=[pltpu.VMEM((nk + rows, MLA_KV_RANK), BF16),
                        pltpu.VMEM((rows, LANES), F32), pltpu.VMEM((rows, LANES), F32),
                        pltpu.VMEM((rows, MLA_KV_RANK), F32),
                        pltpu.VMEM((PAGE_SIZE, MLA_KV_RANK), F32), pltpu.VMEM((PAGE_SIZE, LANES), F32)],
    )
    return pl.pallas_call(
        functools.partial(_mla_sample_kernel, pages_per_step=pps, t=t),
        grid_spec=grid_spec,
        out_shape=jax.ShapeDtypeStruct((nseq, MLA_HEADS, t, MLA_KV_RANK), F32),
        compiler_params=_params(("arbitrary", "arbitrary")),
        name="mla_sample",
    )(page_table, q_abs, q_pe, ckv_new, kpe_new, w_uk_t2, *([cache_ckv] * pps), *([cache_kpe] * pps))


def _gelu(x):
    return 0.5 * x * (1.0 + jnp.tanh(0.7978845608028654 * (x + 0.044715 * (x * x * x))))


def _cmlp_kernel(u_ref, v_ref, gv_ref, bv_ref, w_ref, bs_ref, o_ref, *rest, seg):
    r = u_ref.shape[0]
    u = _gelu(u_ref[...])
    v = _gelu(v_ref[...])
    mu = jnp.mean(v, axis=-1, keepdims=True)
    vc = v - mu
    var = jnp.mean(vc * vc, axis=-1, keepdims=True)
    vn = vc * lax.rsqrt(var + EPS) * gv_ref[...] + bv_ref[...]
    if rest:
        rest[0][...] = vn
    vb = vn.astype(BF16)
    row = lax.broadcasted_iota(jnp.int32, (r, r), 0)
    col = lax.broadcasted_iota(jnp.int32, (r, r), 1)
    keep = jnp.logical_and(col <= row, (row // seg) == (col // seg))
    gw = CMLP_DIM // CMLP_GROUPS
    for g in range(CMLP_GROUPS):
        cs = slice(g * gw, (g + 1) * gw)
        w = jnp.where(keep, w_ref[g], 0.0).astype(BF16)
        mixed = _dot(w, vb[:, cs]) + bs_ref[:, g:g + 1]
        o_ref[:, cs] = (u[:, cs] * mixed).astype(o_ref.dtype)


def chunk_mlp(proj, g_v, b_v, w_mix, b_mix, seg, emit_v):
    m = proj.shape[0]
    d = CMLP_DIM
    const2 = lambda shape: pl.BlockSpec(shape, lambda i: (0, 0))
    out_specs = [pl.BlockSpec((ROW_TILE, d), lambda i: (i, 0))]
    out_shape = [jax.ShapeDtypeStruct((m, d), BF16)]
    if emit_v:
        out_specs.append(pl.BlockSpec((ROW_TILE, d), lambda i: (i, 0)))
        out_shape.append(jax.ShapeDtypeStruct((m, d), F32))
    return pl.pallas_call(
        functools.partial(_cmlp_kernel, seg=seg),
        grid=(m // ROW_TILE,),
        in_specs=[pl.BlockSpec((ROW_TILE, d), lambda i: (i, 0)),
                  pl.BlockSpec((ROW_TILE, d), lambda i: (i, 1)),
                  const2((1, d)), const2((1, d)),
                  pl.BlockSpec((CMLP_GROUPS, ROW_TILE, ROW_TILE), lambda i: (0, 0, 0)),
                  const2((ROW_TILE, CMLP_GROUPS))],
        out_specs=out_specs, out_shape=out_shape,
        compiler_params=_params(("parallel",)),
        name="chunk_mlp",
    )(proj, proj, g_v.reshape(1, d), b_v.reshape(1, d), w_mix, b_mix)


def _rope_tables(pos):
    half = MLA_ROPE // 2
    inv = ROPE_THETA ** (-jnp.arange(half, dtype=F32) / half)
    ang = pos.astype(F32)[:, None] * inv[None, :]
    reps = LANES // half
    return jnp.tile(jnp.cos(ang), (1, reps)), jnp.tile(jnp.sin(ang), (1, reps))


def _layer_c(xp, xs, batch, seq, nseq, t, cache_ckv, cache_kpe, layer, page_table, g_mix, g_ffn, w_in, g_cq, g_ckv,
             w_uq, g_qn, g_qr, g_kn, g_kr, w_uk, w_uv, g_v, b_v, w_s, b_s, w_out, w1, w2):
    past = page_table.shape[1] * PAGE_SIZE
    w_main = jnp.concatenate([w_in[:, C_KPE:], w_in[:, :C_CKV]], axis=1).astype(BF16)
    w_kpe = jnp.pad(w_in[:, C_CKV:C_KPE], ((0, 0), (0, LANES - MLA_ROPE))).astype(BF16)
    wq = w_uq.reshape(MLA_Q_RANK, MLA_HEADS, MLA_NOPE + MLA_ROPE)
    wq_nope = wq[:, :, :MLA_NOPE].reshape(MLA_Q_RANK, -1).astype(BF16)
    wq_rope = wq[:, :, MLA_NOPE:].reshape(MLA_Q_RANK, -1).astype(BF16)
    w_uk2 = w_uk.reshape(MLA_KV_RANK, -1).astype(BF16)
    w_uv2 = w_uv.reshape(MLA_KV_RANK, -1).astype(BF16)
    w_uk_t = w_uk.transpose(1, 2, 0).astype(BF16)
    w_uv_h = w_uv.transpose(1, 0, 2).astype(BF16)
    w_out = w_out.astype(BF16)
    w1 = w1.astype(BF16)
    w2 = w2.astype(BF16)
    ones_row = jnp.ones((MLA_HEADS * MLA_NOPE,), F32)

    pp = norm_matmul(xp, g_mix, w_main)
    kp = norm_matmul(xp, g_mix, w_kpe)
    cos_p, sin_p = _rope_tables(jnp.arange(seq))
    qn, qp, ckv_p, kpe_p, kn, vv, kp2 = mla_prep(pp, kp, cos_p, sin_p, g_cq, g_ckv, jnp.tile(g_qn, MLA_HEADS) * ones_row,
                                                 g_qr, g_kr, wq_nope, wq_rope, (g_kn, w_uk2, w_uv2))
    o_mla_p = mla_prompt(qn, qp, kn, kp2, vv, batch, seq)
    o_cm_p = chunk_mlp(pp, g_v, b_v, w_s, b_s.T, CMLP_CHUNK, False)[0]
    xp = out_proj(o_mla_p, o_cm_p, w_out, xp)
    xp = ffn(xp, g_ffn, w1, w2)
    ckv_p = ckv_p.reshape(batch, seq, MLA_KV_RANK)
    kpe_p = kpe_p[:, :MLA_ROPE].reshape(batch, seq, MLA_ROPE)

    ps = norm_matmul(xs, g_mix, w_main)
    ks = norm_matmul(xs, g_mix, w_kpe)
    rows_s = nseq * t
    tr = min(MLA_PREP_ROWS, rows_s)
    cos_s, sin_s = _rope_tables(past + (jnp.arange(tr) % t))
    qg, qr, ckv_s, kpe_s = mla_prep(ps, ks, cos_s, sin_s, g_cq, g_ckv, jnp.tile(g_qn * g_kn, MLA_HEADS), g_qr, g_kr,
                                    wq_nope, wq_rope)
    q_abs = absorb_queries(qg, w_uk_t, nseq, t)
    q_pe = qr.reshape(nseq, t, MLA_HEADS, MLA_ROPE).transpose(0, 2, 1, 3).reshape(nseq, MLA_HEADS * t, MLA_ROPE)
    o_lat = mla_sample(q_abs, q_pe.astype(F32), ckv_s, kpe_s, w_uk2.T, cache_ckv, cache_kpe, layer, page_table, t)
    o_mla_s = project_latent_out(o_lat, w_uv_h, nseq, t)
    reps = ROW_TILE // t
    w_mix_s = jnp.tile(w_s[:, :t, :t], (1, reps, reps))
    b_mix_s = jnp.tile(b_s[:, :t].T, (reps, 1))
    o_cm_s, cv_s = chunk_mlp(ps, g_v, b_v, w_mix_s, b_mix_s, t, True)
    xs = out_proj(o_mla_s, o_cm_s, w_out, xs)
    xs = ffn(xs, g_ffn, w1, w2)
    ckv_s = ckv_s.reshape(nseq, t, MLA_KV_RANK)
    kpe_s = kpe_s[:, :MLA_ROPE].reshape(nseq, t, MLA_ROPE)
    return xp, xs, (ckv_p, kpe_p), (ckv_s, kpe_s, cv_s.reshape(nseq, t, CMLP_DIM))


def kernel(x_prompt, x_sample, cache_sb_k, cache_sb_v, cache_mla_ckv, cache_mla_kpe, state_ssm, state_conv, page_table, norm_mix, norm_ffn, w_in_a, w_conv, b_conv, dt_bias, a_log, d_skip, g_ssd, w_out_a, w_in_c, g_cq, g_ckv, w_uq, g_qn, g_qr, g_kn, g_kr, w_uk, w_uv, g_v, b_v, w_s, b_s, w_out_c, w_ffn1, w_ffn2):
    batch, seq, dm = x_prompt.shape
    nseq, t, _ = x_sample.shape
    depth = norm_mix.shape[0]
    xp = x_prompt.reshape(batch * seq, dm)
    xs = x_sample.reshape(nseq * t, dm)
    pa, sa, pc, sc = [], [], [], []
    for l in range(depth):
        i = l // 2
        if l % 2 == 0:
            xp, xs, op, os_ = _layer_a(xp, xs, batch, seq, nseq, t, cache_sb_k, cache_sb_v, i, state_ssm[i], state_conv[i],
                                       page_table, norm_mix[l], norm_ffn[l], w_in_a[i], w_conv[i], b_conv[i], dt_bias[i],
                                       a_log[i], d_skip[i], g_ssd[i], w_out_a[i], w_ffn1[l], w_ffn2[l])
            pa.append(op)
            sa.append(os_)
        else:
            xp, xs, op, os_ = _layer_c(xp, xs, batch, seq, nseq, t, cache_mla_ckv, cache_mla_kpe, i, page_table,
                                       norm_mix[l], norm_ffn[l], w_in_c[i], g_cq[i], g_ckv[i], w_uq[i], g_qn[i], g_qr[i],
                                       g_kn[i], g_kr[i], w_uk[i], w_uv[i], g_v[i], b_v[i], w_s[i], b_s[i], w_out_c[i],
                                       w_ffn1[l], w_ffn2[l])
            pc.append(op)
            sc.append(os_)
    stack = lambda items, k: jnp.stack([it[k] for it in items])
    return (xp.reshape(batch, seq, dm), xs.reshape(nseq, t, dm),
            stack(pa, 0), stack(pa, 1), stack(pa, 2), stack(pa, 3), stack(pc, 0), stack(pc, 1),
            stack(sa, 0), stack(sa, 1), stack(sa, 2), stack(sa, 3), stack(sc, 0), stack(sc, 1), stack(sc, 2))
```

```python
import functools

import jax
import jax.numpy as jnp
from jax import lax
from jax.experimental import pallas as pl
from jax.experimental.pallas import tpu as pltpu

F32 = jnp.float32
BF16 = jnp.bfloat16

D_MODEL = 2048
PAGE_SIZE = 128
SSD_D_INNER = D_MODEL
SSD_HEAD_DIM = 64
SSD_HEADS = SSD_D_INNER // SSD_HEAD_DIM
SSD_GROUPS = 8
SSD_STATE = 128
SSD_CONV = 4
SSD_CHUNK = 128
SSD_CONV_DIM = SSD_D_INNER + 2 * SSD_GROUPS * SSD_STATE
SB_HEAD_DIM = 128
SB_HEADS = D_MODEL // SB_HEAD_DIM
SB_KV_HEADS = 4
SB_Q_PER_KV = SB_HEADS // SB_KV_HEADS
SB_SCALE = SB_HEAD_DIM ** -0.5
MLA_HEADS = 16
MLA_Q_RANK = 512
MLA_KV_RANK = 512
MLA_NOPE = 128
MLA_ROPE = 64
MLA_V = 128
MLA_SCALE = (MLA_NOPE + MLA_ROPE) ** -0.5
ROPE_THETA = 10000.0
CMLP_DIM = D_MODEL
CMLP_GROUPS = 8
CMLP_CHUNK = 128
FFN_DIM = 4 * D_MODEL
EPS = 1e-6

A_Z = SSD_D_INNER
A_XBC = A_Z + SSD_CONV_DIM
A_DT = A_XBC + SSD_HEADS
A_Q = A_DT + SB_HEADS * SB_HEAD_DIM
A_K = A_Q + SB_KV_HEADS * SB_HEAD_DIM
C_CQ = MLA_Q_RANK
C_CKV = C_CQ + MLA_KV_RANK
C_KPE = C_CKV + MLA_ROPE
C_U = C_KPE + CMLP_DIM

LANES = 128
ROW_TILE = 128
VMEM_LIMIT = 56 * 1024 * 1024


def _params(semantics, vmem=VMEM_LIMIT, flags=None):
    return pltpu.CompilerParams(dimension_semantics=semantics, vmem_limit_bytes=vmem, flags=flags)


def _split3(x):
    hi = x.astype(BF16)
    r1 = x - hi.astype(F32)
    mid = r1.astype(BF16)
    lo = (r1 - mid.astype(F32)).astype(BF16)
    return hi, mid, lo


def _dot(a, b):
    return jnp.dot(a, b, preferred_element_type=F32)


def _dot_nt(a, b):
    return lax.dot_general(a, b, (((1,), (1,)), ((), ())), preferred_element_type=F32)


def _dot01_right(x, m01):
    hi, mid, lo = _split3(x)
    return _dot(hi, m01) + _dot(mid, m01) + _dot(lo, m01)


def _dot01_left(m01, x):
    hi, mid, lo = _split3(x)
    return _dot(m01, hi) + _dot(m01, mid) + _dot(m01, lo)


def _softplus(x):
    return jnp.maximum(x, 0.0) + jnp.log1p(jnp.exp(-jnp.abs(x)))


def _silu(x):
    return x * jax.nn.sigmoid(x)


def _rms(x, g):
    ms = jnp.mean(x * x, axis=-1, keepdims=True)
    return x * lax.rsqrt(ms + EPS) * g


def _norm_matmul_kernel(x_ref, g_ref, w_ref, o_ref, h_ref):
    @pl.when(pl.program_id(1) == 0)
    def _():
        h_ref[...] = _rms(x_ref[...], g_ref[...]).astype(BF16)

    o_ref[...] = _dot(h_ref[...], w_ref[...])


def norm_matmul(x, g, w, tm=1024, tn=512):
    m, k = x.shape
    n = w.shape[1]
    tm, tn = min(tm, m), min(tn, n)
    return pl.pallas_call(
        _norm_matmul_kernel,
        grid=(m // tm, n // tn),
        in_specs=[pl.BlockSpec((tm, k), lambda i, j: (i, 0)),
                  pl.BlockSpec((1, k), lambda i, j: (0, 0)),
                  pl.BlockSpec((k, tn), lambda i, j: (0, j))],
        out_specs=pl.BlockSpec((tm, tn), lambda i, j: (i, j)),
        out_shape=jax.ShapeDtypeStruct((m, n), F32),
        scratch_shapes=[pltpu.VMEM((tm, k), BF16)],
        compiler_params=_params(("parallel", "arbitrary")),
        name="norm_matmul",
    )(x, g.reshape(1, k), w)


def _out_proj_kernel(a_ref, b_ref, wa_ref, wb_ref, r_ref, o_ref):
    o_ref[...] = r_ref[...] + (_dot(a_ref[...], wa_ref[...]) + _dot(b_ref[...], wb_ref[...]))


def out_proj(a, b, w, res, tm=1024, tn=512):
    m, kh = a.shape
    n = w.shape[1]
    tm, tn = min(tm, m), min(tn, n)
    return pl.pallas_call(
        _out_proj_kernel,
        grid=(m // tm, n // tn),
        in_specs=[pl.BlockSpec((tm, kh), lambda i, j: (i, 0)),
                  pl.BlockSpec((tm, kh), lambda i, j: (i, 0)),
                  pl.BlockSpec((kh, tn), lambda i, j: (0, j)),
                  pl.BlockSpec((kh, tn), lambda i, j: (1, j)),
                  pl.BlockSpec((tm, tn), lambda i, j: (i, j))],
        out_specs=pl.BlockSpec((tm, tn), lambda i, j: (i, j)),
        out_shape=jax.ShapeDtypeStruct((m, n), F32),
        compiler_params=_params(("parallel", "arbitrary")),
        name="out_proj",
    )(a, b, w, w, res)


def _ffn_kernel(x_ref, g_ref, w1_ref, w2_ref, o_ref, h_ref):
    @pl.when(pl.program_id(1) == 0)
    def _():
        x = x_ref[...]
        h_ref[...] = _rms(x, g_ref[...]).astype(BF16)
        o_ref[...] = x

    a = jnp.maximum(_dot(h_ref[...], w1_ref[...]), 0.0)
    o_ref[...] += _dot((a * a).astype(BF16), w2_ref[...])


def ffn(x, g, w1, w2, tm=1024, tf=512):
    m, d = x.shape
    f = w1.shape[1]
    tm, tf = min(tm, m), min(tf, f)
    return pl.pallas_call(
        _ffn_kernel,
        grid=(m // tm, f // tf),
        in_specs=[pl.BlockSpec((tm, d), lambda i, j: (i, 0)),
                  pl.BlockSpec((1, d), lambda i, j: (0, 0)),
                  pl.BlockSpec((d, tf), lambda i, j: (0, j)),
                  pl.BlockSpec((tf, d), lambda i, j: (j, 0))],
        out_specs=pl.BlockSpec((tm, d), lambda i, j: (i, 0)),
        out_shape=jax.ShapeDtypeStruct((m, d), F32),
        scratch_shapes=[pltpu.VMEM((tm, d), BF16)],
        compiler_params=_params(("parallel", "arbitrary")),
        name="ffn",
    )(x, g.reshape(1, d), w1, w2)


def _head_expand():
    h = lax.broadcasted_iota(jnp.int32, (LANES, SSD_D_INNER), 0)
    c = lax.broadcasted_iota(jnp.int32, (LANES, SSD_D_INNER), 1)
    return (c // SSD_HEAD_DIM == h).astype(BF16)


def _conv_silu(pad_ref, w_ref, b_ref, lead, rows):
    acc = b_ref[...] + w_ref[SSD_CONV - 1:SSD_CONV, :] * pad_ref[lead:lead + rows, :]
    for k in range(SSD_CONV - 1):
        s = lead - (SSD_CONV - 1) + k
        acc = acc + w_ref[k:k + 1, :] * pad_ref[s:s + rows, :]
    return _silu(acc)


def _ssd_intra(xs, bc, z_unused, dt_raw, dtb, alog, e01, same_block):
    r = xs.shape[0]
    row = lax.broadcasted_iota(jnp.int32, (r, r), 0)
    col = lax.broadcasted_iota(jnp.int32, (r, r), 1)
    causal = jnp.logical_and(col <= row, same_block)
    dt = _softplus(dt_raw + dtb)
    dta = dt * (-jnp.exp(alog))
    at = _dot01_left(causal.astype(BF16), dta)
    at_t = at.T
    at_x = _dot01_right(at, e01)
    dt_x = _dot01_right(dt, e01)
    xdt = xs * dt_x
    return dict(causal=causal, at=at, at_t=at_t, at_x=at_x, xdt=xdt)


def _ssd_diag_pair(q, pr, cb, at, at_t, causal, xdt_bf):
    lane = lax.broadcasted_iota(jnp.int32, (q, LANES), 1)
    ys = []
    for h in (2 * pr, 2 * pr + 1):
        d = at[:, h:h + 1] - at_t[h:h + 1, :]
        dec = jnp.exp(jnp.where(causal, d, -jnp.inf))
        ys.append(_dot((cb * dec).astype(BF16), xdt_bf))
    return jnp.where(lane < SSD_HEAD_DIM, ys[0], ys[1])


def _gated_group_norm(y, z, g_ref, o_ref):
    gw = SSD_D_INNER // SSD_GROUPS
    for g in range(SSD_GROUPS):
        cs = slice(g * gw, (g + 1) * gw)
        yg = y[:, cs] * _silu(z[:, cs])
        o_ref[:, cs] = _rms(yg, g_ref[:, cs]).astype(o_ref.dtype)


def _ssd_prompt_kernel(z_ref, xs_ref, bc_ref, dt_ref, wcx_ref, wcb_ref, bcx_ref, bcb_ref, dtb_ref,
                       alog_ref, dsk_ref, gss_ref, e_ref, y_ref, s_ref, padx, padb, st, ybuf):
    c = pl.program_id(1)
    q = ROW_TILE
    lead = 8
    gn = SSD_GROUPS * SSD_STATE

    @pl.when(c == 0)
    def _():
        padx[0:lead, :] = jnp.zeros((lead, SSD_D_INNER), F32)
        padb[0:lead, :] = jnp.zeros((lead, 2 * gn), F32)
        st[...] = jnp.zeros(st.shape, F32)

    padx[lead:lead + q, :] = xs_ref[...]
    padb[lead:lead + q, :] = bc_ref[...]
    xs = _conv_silu(padx, wcx_ref, bcx_ref, lead, q)
    bc = _conv_silu(padb, wcb_ref, bcb_ref, lead, q)
    padx[0:lead, :] = padx[q:q + lead, :]
    padb[0:lead, :] = padb[q:q + lead, :]

    e01 = e_ref[...]
    whole = jnp.full((q, q), True)
    r = _ssd_intra(xs, bc, None, dt_ref[...], dtb_ref[...], alog_ref[...], e01, whole)
    at, at_t, at_x, xdt, causal = r["at"], r["at_t"], r["at_x"], r["xdt"], r["causal"]
    eat = jnp.exp(at_x)
    last = at_x[q - 1:q, :]
    xw = (xdt * jnp.exp(last - at_x)).astype(BF16)
    cd = jnp.exp(last)
    xdt_bf = xdt.astype(BF16)

    for g in range(SSD_GROUPS):
        bm = bc[:, g * SSD_STATE:(g + 1) * SSD_STATE]
        cm = bc[:, gn + g * SSD_STATE:gn + (g + 1) * SSD_STATE].astype(BF16)
        cb = _dot_nt(cm, bm.astype(BF16))
        bm_t = bm.T.astype(BF16)
        for pr in (2 * g, 2 * g + 1):
            cs = slice(pr * LANES, (pr + 1) * LANES)
            yd = _ssd_diag_pair(q, pr, cb, at, at_t, causal, xdt_bf[:, cs])
            s_in = st[:, cs]
            yo = _dot(cm, s_in.astype(BF16)) * eat[:, cs]
            st[:, cs] = s_in * cd[:, cs] + _dot(bm_t, xw[:, cs])
            ybuf[:, cs] = yd + yo

    y = ybuf[...] + dsk_ref[...] * xs
    _gated_group_norm(y, z_ref[...], gss_ref, y_ref)

    @pl.when(c == pl.num_programs(1) - 1)
    def _():
        for pr in range(SSD_D_INNER // LANES):
            cs = slice(pr * LANES, (pr + 1) * LANES)
            s_ref[0, cs, :] = st[:, cs].T


def ssd_prompt(proj, dtp, batch, seq, w_conv, b_conv, dt_bias, a_log, d_skip, g_ssd):
    nc = seq // ROW_TILE
    gn = SSD_GROUPS * SSD_STATE
    d = SSD_D_INNER
    row = lambda b, c: b * nc + c
    const = lambda shape: pl.BlockSpec(shape, lambda b, c: (0, 0))
    pad128 = lambda v: jnp.pad(v, (0, LANES - v.shape[0])).reshape(1, LANES)
    return pl.pallas_call(
        _ssd_prompt_kernel,
        grid=(batch, nc),
        in_specs=[pl.BlockSpec((ROW_TILE, d), lambda b, c: (row(b, c), 0)),
                  pl.BlockSpec((ROW_TILE, d), lambda b, c: (row(b, c), 1)),
                  pl.BlockSpec((ROW_TILE, 2 * gn), lambda b, c: (row(b, c), 2)),
                  pl.BlockSpec((ROW_TILE, LANES), lambda b, c: (row(b, c), 0)),
                  const((SSD_CONV, d)), const((SSD_CONV, 2 * gn)), const((1, d)), const((1, 2 * gn)),
                  const((1, LANES)), const((1, LANES)), const((1, d)), const((1, d)),
                  const((LANES, d))],
        out_specs=[pl.BlockSpec((ROW_TILE, d), lambda b, c: (row(b, c), 0)),
                   pl.BlockSpec((1, d, SSD_STATE), lambda b, c: (b, 0, 0))],
        out_shape=[jax.ShapeDtypeStruct((batch * seq, d), BF16),
                   jax.ShapeDtypeStruct((batch, d, SSD_STATE), F32)],
        scratch_shapes=[pltpu.VMEM((ROW_TILE + 8, d), F32), pltpu.VMEM((ROW_TILE + 8, 2 * gn), F32),
                        pltpu.VMEM((SSD_STATE, d), F32), pltpu.VMEM((ROW_TILE, d), F32)],
        compiler_params=_params(("parallel", "arbitrary")),
        name="ssd_prompt",
    )(proj, proj, proj, dtp, w_conv[:, :d], w_conv[:, d:], b_conv[:d].reshape(1, d),
      b_conv[d:].reshape(1, 2 * gn), pad128(dt_bias), pad128(a_log),
      jnp.repeat(d_skip, SSD_HEAD_DIM).reshape(1, d), g_ssd.reshape(1, d), _head_expand())


def _ssd_sample_kernel(z_ref, xs_ref, bc_ref, dt_ref, cvx_ref, cvb_ref, s0_ref, wcx_ref, wcb_ref,
                       bcx_ref, bcb_ref, dtb_ref, alog_ref, dsk_ref, gss_ref, e_ref, et_ref,
                       y_ref, cox_ref, cob_ref, s_ref,
                       padx, padb, ybuf, yoff, eat_s, cm_s, bm_s, xwt_s, att_s):
    j = pl.program_id(1)
    r = ROW_TILE
    t = xs_ref.shape[0] // cvx_ref.shape[0]
    nseq = r // t
    lead = 8
    gn = SSD_GROUPS * SSD_STATE
    d = SSD_D_INNER
    hist = SSD_CONV - 1

    @pl.when(j == 0)
    def _():
        def conv(pad, raw_ref, cv_ref, w_ref, b_ref, co_ref):
            width = raw_ref.shape[1]
            pad[:, lead - hist:lead, :] = cv_ref[...]
            pad[:, lead:lead + t, :] = raw_ref[...].reshape(nseq, t, width)
            acc = b_ref[...] + w_ref[hist:hist + 1, :] * pad[:, lead:lead + t, :]
            for k in range(hist):
                s = lead - hist + k
                acc = acc + w_ref[k:k + 1, :] * pad[:, s:s + t, :]
            co_ref[...] = pad[:, lead + t - hist:lead + t, :]
            return _silu(acc).reshape(r, width)

        xs = conv(padx, xs_ref, cvx_ref, wcx_ref, bcx_ref, cox_ref)
        bc = conv(padb, bc_ref, cvb_ref, wcb_ref, bcb_ref, cob_ref)
        e01 = e_ref[...]
        row = lax.broadcasted_iota(jnp.int32, (r, r), 0)
        col = lax.broadcasted_iota(jnp.int32, (r, r), 1)
        same = (row // t) == (col // t)
        q = _ssd_intra(xs, bc, None, dt_ref[...], dtb_ref[...], alog_ref[...], e01, same)
        at, at_t, at_x, xdt, causal = q["at"], q["at_t"], q["at_x"], q["xdt"], q["causal"]
        eat_s[...] = jnp.exp(at_x)
        last_sel = jnp.logical_and(same, (col % t) == t - 1).astype(BF16)
        last = _dot01_left(last_sel, at_x)
        xw = xdt * jnp.exp(last - at_x)
        xdt_bf = xdt.astype(BF16)
        att_s[...] = _dot01_left(et_ref[...], at_t)
        for g in range(SSD_GROUPS):
            bm = bc[:, g * SSD_STATE:(g + 1) * SSD_STATE].astype(BF16)
            cm = bc[:, gn + g * SSD_STATE:gn + (g + 1) * SSD_STATE].astype(BF16)
            bm_s[:, g * SSD_STATE:(g + 1) * SSD_STATE] = bm
            cm_s[:, g * SSD_STATE:(g + 1) * SSD_STATE] = cm
            cb = _dot_nt(cm, bm)
            for pr in (2 * g, 2 * g + 1):
                cs = slice(pr * LANES, (pr + 1) * LANES)
                ybuf[:, cs] = _ssd_diag_pair(r, pr, cb, at, at_t, causal, xdt_bf[:, cs])
                xwt_s[cs, :] = xw[:, cs].T.astype(BF16)
        ybuf[...] = ybuf[...] + dsk_ref[...] * xs
        yoff[...] = jnp.zeros(yoff.shape, F32)

    rowseq = lax.broadcasted_iota(jnp.int32, (r, LANES), 0) // t
    laneseq = lax.broadcasted_iota(jnp.int32, (LANES, r), 1) // t
    lastlane = lax.broadcasted_iota(jnp.int32, (LANES, r), 1) == j * t + t - 1
    gw = d // SSD_GROUPS
    for g in range(SSD_GROUPS):
        rs = slice(g * gw, (g + 1) * gw)
        gs = slice(g * SSD_STATE, (g + 1) * SSD_STATE)
        s0 = s0_ref[0, rs, :]
        yo = _dot_nt(cm_s[:, gs], s0.astype(BF16))
        for half in range(gw // LANES):
            cs = slice(g * gw + half * LANES, g * gw + (half + 1) * LANES)
            yoff[:, cs] += jnp.where(rowseq == j, yo[:, half * LANES:(half + 1) * LANES], 0.0)
        sel = jnp.concatenate([laneseq, laneseq], axis=0) == j
        sc = _dot(jnp.where(sel, xwt_s[rs, :], jnp.zeros((), BF16)), bm_s[:, gs])
        lastc = jnp.concatenate([lastlane, lastlane], axis=0)
        cdc = jnp.exp(jnp.sum(jnp.where(lastc, att_s[rs, :], 0.0), axis=1, keepdims=True))
        s_ref[0, rs, :] = s0 * cdc + sc

    @pl.when(j == pl.num_programs(1) - 1)
    def _():
        y = ybuf[...] + yoff[...] * eat_s[...]
        _gated_group_norm(y, z_ref[...], gss_ref, y_ref)


def ssd_sample(proj, dtp, nseq, t, state_conv, state_ssm, w_conv, b_conv, dt_bias, a_log, d_skip, g_ssd):
    spb = ROW_TILE // t
    nb = nseq // spb
    gn = SSD_GROUPS * SSD_STATE
    d = SSD_D_INNER
    hist = SSD_CONV - 1
    const = lambda shape: pl.BlockSpec(shape, lambda i, j: (0, 0))
    pad128 = lambda v: jnp.pad(v, (0, LANES - v.shape[0])).reshape(1, LANES)
    e01 = _head_expand()
    outs = pl.pallas_call(
        _ssd_sample_kernel,
        grid=(nb, spb),
        in_specs=[pl.BlockSpec((ROW_TILE, d), lambda i, j: (i, 0)),
                  pl.BlockSpec((ROW_TILE, d), lambda i, j: (i, 1)),
                  pl.BlockSpec((ROW_TILE, 2 * gn), lambda i, j: (i, 2)),
                  pl.BlockSpec((ROW_TILE, LANES), lambda i, j: (i, 0)),
                  pl.BlockSpec((spb, hist, d), lambda i, j: (i, 0, 0)),
                  pl.BlockSpec((spb, hist, 2 * gn), lambda i, j: (i, 0, 1)),
                  pl.BlockSpec((1, d, SSD_STATE), lambda i, j: (i * spb + j, 0, 0)),
                  const((SSD_CONV, d)), const((SSD_CONV, 2 * gn)), const((1, d)), const((1, 2 * gn)),
                  const((1, LANES)), const((1, LANES)), const((1, d)), const((1, d)),
                  const((LANES, d)), const((d, LANES))],
        out_specs=[pl.BlockSpec((ROW_TILE, d), lambda i, j: (i, 0)),
                   pl.BlockSpec((spb, hist, d), lambda i, j: (i, 0, 0)),
                   pl.BlockSpec((spb, hist, 2 * gn), lambda i, j: (i, 0, 0)),
                   pl.BlockSpec((1, d, SSD_STATE), lambda i, j: (i * spb + j, 0, 0))],
        out_shape=[jax.ShapeDtypeStruct((nseq * t, d), BF16),
                   jax.ShapeDtypeStruct((nseq, hist, d), F32),
                   jax.ShapeDtypeStruct((nseq, hist, 2 * gn), F32),
                   jax.ShapeDtypeStruct((nseq, d, SSD_STATE), F32)],
        scratch_shapes=[pltpu.VMEM((spb, 8 + t, d), F32), pltpu.VMEM((spb, 8 + t, 2 * gn), F32),
                        pltpu.VMEM((ROW_TILE, d), F32), pltpu.VMEM((ROW_TILE, d), F32),
                        pltpu.VMEM((ROW_TILE, d), F32),
                        pltpu.VMEM((ROW_TILE, gn), BF16), pltpu.VMEM((ROW_TILE, gn), BF16),
                        pltpu.VMEM((d, ROW_TILE), BF16), pltpu.VMEM((d, ROW_TILE), F32)],
        compiler_params=_params(("parallel", "arbitrary")),
        name="ssd_sample",
    )(proj, proj, proj, dtp, state_conv, state_conv, state_ssm,
      w_conv[:, :d], w_conv[:, d:], b_conv[:d].reshape(1, d), b_conv[d:].reshape(1, 2 * gn),
      pad128(dt_bias), pad128(a_log), jnp.repeat(d_skip, SSD_HEAD_DIM).reshape(1, d),
      g_ssd.reshape(1, d), e01, e01.T)
    y, cox, cob, s_new = outs
    return y, jnp.concatenate([cox, cob], axis=-1), s_new


def _later_matrix():
    j = lax.broadcasted_iota(jnp.int32, (2 * LANES, 2 * LANES), 0) % LANES
    s = lax.broadcasted_iota(jnp.int32, (2 * LANES, 2 * LANES), 1)
    return jnp.logical_or(j > s, s >= LANES).astype(BF16)


def _sb_weights(zs, vis, carry, later01):
    assert vis is None or len(zs) == 1
    rows = zs[0].shape[0]
    z = jnp.concatenate(zs, axis=0)
    sp = _softplus(z)
    lf = -sp
    ls = z - sp
    if vis is not None:
        lf = jnp.where(vis, lf, 0.0)
    hi = lf.astype(BF16)
    lo = (lf - hi.astype(F32)).astype(BF16)
    lc = _dot(jnp.concatenate([hi, lo], axis=1), later01)
    ws = []
    for i in range(len(zs)):
        rs = slice(i * rows, (i + 1) * rows)
        w = jnp.exp(ls[rs] + lc[rs, :LANES] + carry)
        ws.append(w if vis is None else jnp.where(vis, w, 0.0))
        carry = carry + lc[rs, LANES:]
    return ws, carry


def _sb_prompt_kernel(q_ref, k_ref, v_ref, o_ref, acc, carry):
    qi = pl.program_id(2)
    t = ROW_TILE
    g4 = SB_Q_PER_KV
    later01 = _later_matrix()
    q = q_ref[...]
    qs = jnp.concatenate([q[:, g * SB_HEAD_DIM:(g + 1) * SB_HEAD_DIM] for g in range(g4)], axis=0)
    qs = (qs * SB_SCALE).astype(BF16)
    rows = g4 * t
    chunks = 2
    hr = rows // chunks

    def kv_rows(j):
        return pl.ds(j * t, t) if isinstance(j, int) else pl.ds(pl.multiple_of(j * t, t), t)

    def step(js, vis):
        ks = [k_ref[kv_rows(j), :].astype(BF16) for j in js]
        vs = jnp.concatenate([v_ref[kv_rows(j), :].astype(BF16) for j in js], axis=0)
        zs = [[_dot_nt(qs[c * hr:(c + 1) * hr], k) for k in ks] for c in range(chunks)]
        wcat = []
        for c in range(chunks):
            rs = slice(c * hr, (c + 1) * hr)
            ws, cn = _sb_weights(zs[c], vis, carry[rs, :], later01)
            carry[rs, :] = cn
            wcat.append(jnp.concatenate([w.astype(BF16) for w in ws], axis=1))
        for c in range(chunks):
            acc[c * hr:(c + 1) * hr, :] += _dot(wcat[c], vs)

    acc[...] = jnp.zeros(acc.shape, F32)
    carry[...] = jnp.zeros(carry.shape, F32)
    row_t = lax.broadcasted_iota(jnp.int32, (hr, t), 0) % t
    col = lax.broadcasted_iota(jnp.int32, (hr, t), 1)
    step([qi], col < row_t)

    def body(i, _):
        j = qi - 1 - 2 * i
        step([j, j - 1], None)
        return 0

    lax.fori_loop(0, qi // 2, body, 0)

    @pl.when(qi % 2 == 1)
    def _():
        step([0], None)

    for g in range(g4):
        o_ref[:, g * SB_HEAD_DIM:(g + 1) * SB_HEAD_DIM] = acc[g * t:(g + 1) * t, :].astype(o_ref.dtype)


def sb_prompt(proj, batch, seq):
    nq = seq // ROW_TILE
    qw = SB_Q_PER_KV * SB_HEAD_DIM
    q0 = (A_XBC) // qw
    k0 = (A_XBC + SB_HEADS * SB_HEAD_DIM) // SB_HEAD_DIM
    v0 = k0 + SB_KV_HEADS
    return pl.pallas_call(
        _sb_prompt_kernel,
        grid=(batch, SB_KV_HEADS, nq),
        in_specs=[pl.BlockSpec((ROW_TILE, qw), lambda b, h, i: (b * nq + i, q0 + h)),
                  pl.BlockSpec((seq, SB_HEAD_DIM), lambda b, h, i: (b, k0 + h)),
                  pl.BlockSpec((seq, SB_HEAD_DIM), lambda b, h, i: (b, v0 + h))],
        out_specs=pl.BlockSpec((ROW_TILE, qw), lambda b, h, i: (b * nq + i, h)),
        out_shape=jax.ShapeDtypeStruct((batch * seq, SB_HEADS * SB_HEAD_DIM), BF16),
        scratch_shapes=[pltpu.VMEM((SB_Q_PER_KV * ROW_TILE, SB_HEAD_DIM), F32),
                        pltpu.VMEM((SB_Q_PER_KV * ROW_TILE, LANES), F32)],
        compiler_params=_params(("parallel", "parallel", "arbitrary")),
        name="sb_prompt",
    )(proj, proj, proj)


def _sb_sample_kernel(pt_ref, q_ref, kn_ref, vn_ref, *refs, pages_per_step, t):
    k_refs = refs[:pages_per_step]
    v_refs = refs[pages_per_step:2 * pages_per_step]
    o_ref, acc, carry, knp, vnp = refs[2 * pages_per_step:]
    p = pl.program_id(1)
    rows = q_ref.shape[1]
    rk = rows // SB_KV_HEADS
    later01 = _later_matrix()
    qs = (q_ref[0] * SB_SCALE).astype(BF16)

    heads = range(SB_KV_HEADS)
    hrow = lambda h: slice(h * rk, (h + 1) * rk)

    def blocks(kfns, vfns, vis):
        zs = [jnp.concatenate([_dot_nt(qs[hrow(h)], kfn(h)) for h in heads], axis=0) for kfn in kfns]
        ws, c = _sb_weights(zs, vis, carry[...], later01)
        carry[...] = c
        ws = [w.astype(BF16) for w in ws]
        outs = []
        for h in heads:
            wh = jnp.concatenate([w[hrow(h)] for w in ws], axis=1)
            vh = jnp.concatenate([vfn(h) for vfn in vfns], axis=0)
            outs.append(_dot(wh, vh))
        acc[...] += jnp.concatenate(outs, axis=0)

    @pl.when(p == 0)
    def _():
        acc[...] = jnp.zeros(acc.shape, F32)
        carry[...] = jnp.zeros(carry.shape, F32)
        knp[...] = jnp.zeros(knp.shape, F32)
        vnp[...] = jnp.zeros(vnp.shape, F32)
        knp[0:t, :] = kn_ref[...]
        vnp[0:t, :] = vn_ref[...]
        row_t = lax.broadcasted_iota(jnp.int32, (rows, LANES), 0) % t
        col = lax.broadcasted_iota(jnp.int32, (rows, LANES), 1)
        hd = lambda ref: (lambda h: ref[:, h * SB_HEAD_DIM:(h + 1) * SB_HEAD_DIM].astype(BF16))
        blocks([hd(knp)], [hd(vnp)], col < row_t)

    head = lambda ref: (lambda h: ref[pl.ds(h, PAGE_SIZE, stride=SB_KV_HEADS), :].astype(BF16))
    blocks([head(r) for r in k_refs], [head(r) for r in v_refs], None)

    @pl.when(p == pl.num_programs(1) - 1)
    def _():
        o_ref[0] = acc[...].astype(o_ref.dtype)


def sb_sample(q_rows, proj, cache_k, cache_v, layer, page_table, t, pages_per_step=16):
    nseq, n_pages = page_table.shape
    pps = min(pages_per_step, n_pages)
    nsteps = n_pages // pps
    kw = SB_KV_HEADS * SB_HEAD_DIM
    k0 = (A_XBC + SB_HEADS * SB_HEAD_DIM) // kw
    rows = q_rows.shape[1]
    page_rows = PAGE_SIZE * SB_KV_HEADS
    cache_k = cache_k.reshape(cache_k.shape[:2] + (page_rows, SB_HEAD_DIM))
    cache_v = cache_v.reshape(cache_v.shape[:2] + (page_rows, SB_HEAD_DIM))

    def page_spec(i):
        return pl.BlockSpec((None, None, page_rows, SB_HEAD_DIM),
                            lambda b, p, pt: (layer, pt[b, n_pages - 1 - (p * pps + i)], 0, 0))

    grid_spec = pltpu.PrefetchScalarGridSpec(
        num_scalar_prefetch=1,
        grid=(nseq, nsteps),
        in_specs=[pl.BlockSpec((1, rows, SB_HEAD_DIM), lambda b, p, pt: (b, 0, 0)),
                  pl.BlockSpec((t, kw), lambda b, p, pt: (b, k0)),
                  pl.BlockSpec((t, kw), lambda b, p, pt: (b, k0 + 1))]
                 + [page_spec(i) for i in range(pps)] + [page_spec(i) for i in range(pps)],
        out_specs=pl.BlockSpec((1, rows, SB_HEAD_DIM), lambda b, p, pt: (b, 0, 0)),
        scratch_shapes=[pltpu.VMEM((rows, SB_HEAD_DIM), F32), pltpu.VMEM((rows, LANES), F32),
                        pltpu.VMEM((PAGE_SIZE, kw), F32), pltpu.VMEM((PAGE_SIZE, kw), F32)],
    )
    return pl.pallas_call(
        functools.partial(_sb_sample_kernel, pages_per_step=pps, t=t),
        grid_spec=grid_spec,
        out_shape=jax.ShapeDtypeStruct((nseq, rows, SB_HEAD_DIM), BF16),
        compiler_params=_params(("parallel", "arbitrary")),
        name="sb_sample",
    )(page_table, q_rows, proj, proj, *([cache_k] * pps), *([cache_v] * pps))


def _layer_a(xp, xs, batch, seq, nseq, t, cache_k, cache_v, layer, state_ssm, state_conv, page_table,
             g_mix, g_ffn, w_in, w_conv, b_conv, dt_bias, a_log, d_skip, g_ssd, w_out, w1, w2):
    w_main = jnp.concatenate([w_in[:, :A_XBC], w_in[:, A_DT:]], axis=1).astype(BF16)
    w_dt = jnp.pad(w_in[:, A_XBC:A_DT], ((0, 0), (0, LANES - SSD_HEADS))).astype(BF16)
    w_out = w_out.astype(BF16)
    w1 = w1.astype(BF16)
    w2 = w2.astype(BF16)
    gn2 = 2 * SSD_GROUPS * SSD_STATE
    qcol = A_XBC
    kcol = qcol + SB_HEADS * SB_HEAD_DIM
    vcol = kcol + SB_KV_HEADS * SB_HEAD_DIM

    pp = norm_matmul(xp, g_mix, w_main)
    dp = norm_matmul(xp, g_mix, w_dt)
    y_p, ssm_p = ssd_prompt(pp, dp, batch, seq, w_conv, b_conv, dt_bias, a_log, d_skip, g_ssd)
    o_p = sb_prompt(pp, batch, seq)
    xp = out_proj(y_p, o_p, w_out, xp)
    xp = ffn(xp, g_ffn, w1, w2)
    sbk_p = pp[:, kcol:vcol].reshape(batch, seq, SB_KV_HEADS, SB_HEAD_DIM)
    sbv_p = pp[:, vcol:].reshape(batch, seq, SB_KV_HEADS, SB_HEAD_DIM)
    conv_p = pp.reshape(batch, seq, -1)[:, seq - (SSD_CONV - 1):, A_Z:A_Z + SSD_CONV_DIM]
    ssm_p = ssm_p.reshape(batch, SSD_HEADS, SSD_HEAD_DIM, SSD_STATE)

    ps = norm_matmul(xs, g_mix, w_main)
    ds = norm_matmul(xs, g_mix, w_dt)
    y_s, conv_s, ssm_s = ssd_sample(ps, ds, nseq, t, state_conv, state_ssm.reshape(nseq, SSD_D_INNER, SSD_STATE),
                                    w_conv, b_conv, dt_bias, a_log, d_skip, g_ssd)
    q_rows = ps[:, qcol:kcol].reshape(nseq, t, SB_KV_HEADS, SB_Q_PER_KV, SB_HEAD_DIM)
    q_rows = q_rows.transpose(0, 2, 3, 1, 4).reshape(nseq, SB_HEADS * t, SB_HEAD_DIM)
    o_s = sb_sample(q_rows, ps, cache_k, cache_v, layer, page_table, t)
    o_s = o_s.reshape(nseq, SB_KV_HEADS, SB_Q_PER_KV, t, SB_HEAD_DIM).transpose(0, 3, 1, 2, 4)
    o_s = o_s.reshape(nseq * t, SB_HEADS * SB_HEAD_DIM)
    xs = out_proj(y_s, o_s, w_out, xs)
    xs = ffn(xs, g_ffn, w1, w2)
    sbk_s = ps[:, kcol:vcol].reshape(nseq, t, SB_KV_HEADS, SB_HEAD_DIM)
    sbv_s = ps[:, vcol:].reshape(nseq, t, SB_KV_HEADS, SB_HEAD_DIM)
    ssm_s = ssm_s.reshape(nseq, SSD_HEADS, SSD_HEAD_DIM, SSD_STATE)
    return xp, xs, (sbk_p, sbv_p, ssm_p, conv_p), (sbk_s, sbv_s, ssm_s, conv_s)


MLA_PREP_ROWS = 256
C_MAIN = 2 * CMLP_DIM + MLA_Q_RANK + MLA_KV_RANK


def _half_head_ones():
    a = lax.broadcasted_iota(jnp.int32, (LANES, LANES), 0) // MLA_ROPE
    b = lax.broadcasted_iota(jnp.int32, (LANES, LANES), 1) // MLA_ROPE
    return (a == b).astype(BF16)


def _rope_norm(x, g, cos, sin, ones64):
    ms = _dot01_right(x * x, ones64) * (1.0 / MLA_ROPE)
    xn = x * lax.rsqrt(ms + EPS) * g
    lane = lax.broadcasted_iota(jnp.int32, x.shape, 1) % MLA_ROPE
    rot = jnp.where(lane < MLA_ROPE // 2, -pltpu.roll(xn, LANES - MLA_ROPE // 2, 1), pltpu.roll(xn, MLA_ROPE // 2, 1))
    return xn * cos + rot * sin


def _mla_prep_kernel(cq_ref, ckv_ref, kpe_ref, cos_ref, sin_ref, gcq_ref, gckv_ref, gqn_ref, gqr_ref, gkr_ref,
                     wqn_ref, wqr_ref, *rest, with_kv):
    if with_kv:
        gkn_ref, wuk_ref, wuv_ref, qn_ref, qp_ref, ckvn_ref, kper_ref, kn_ref, vv_ref, kp2_ref = rest
    else:
        qn_ref, qp_ref, ckvn_ref, kper_ref = rest
    ones64 = _half_head_ones()
    cos, sin = cos_ref[...], sin_ref[...]
    cqn = _rms(cq_ref[...], gcq_ref[...]).astype(BF16)
    qn = _dot(cqn, wqn_ref[...])
    qp = _dot(cqn, wqr_ref[...])
    for h in range(MLA_HEADS):
        cs = slice(h * MLA_NOPE, (h + 1) * MLA_NOPE)
        qn_ref[:, cs] = (_rms(qn[:, cs], gqn_ref[:, cs]) * MLA_SCALE).astype(qn_ref.dtype)
    for b in range(MLA_HEADS * MLA_ROPE // LANES):
        cs = slice(b * LANES, (b + 1) * LANES)
        qp_ref[:, cs] = (_rope_norm(qp[:, cs], gqr_ref[...], cos, sin, ones64) * MLA_SCALE).astype(qp_ref.dtype)
    ckvn = _rms(ckv_ref[...], gckv_ref[...])
    ckvn_ref[...] = ckvn
    kper = _rope_norm(kpe_ref[...], gkr_ref[...], cos, sin, ones64)
    kper_ref[...] = kper
    if with_kv:
        cb = ckvn.astype(BF16)
        kraw = _dot(cb, wuk_ref[...])
        for h in range(MLA_HEADS):
            cs = slice(h * MLA_NOPE, (h + 1) * MLA_NOPE)
            kn_ref[:, cs] = _rms(kraw[:, cs], gkn_ref[...]).astype(kn_ref.dtype)
        vv_ref[...] = _dot(cb, wuv_ref[...]).astype(vv_ref.dtype)
        kp2_ref[...] = (kper + pltpu.roll(kper, MLA_ROPE, 1)).astype(kp2_ref.dtype)


def mla_prep(proj, kpe_raw, cos, sin, g_cq, g_ckv, g_qn_row, g_qr, g_kr, wq_nope, wq_rope, kv_weights=None):
    m = proj.shape[0]
    tr = min(MLA_PREP_ROWS, m)
    ncos = cos.shape[0] // tr
    c0 = 2 * CMLP_DIM // MLA_Q_RANK
    hq = MLA_HEADS * MLA_NOPE
    hr = MLA_HEADS * MLA_ROPE
    with_kv = kv_weights is not None
    const = lambda shape: pl.BlockSpec(shape, lambda i: (0, 0))
    row = lambda w: pl.BlockSpec((tr, w), lambda i: (i, 0))
    pad_row = lambda v: jnp.concatenate([v, jnp.zeros((LANES - v.shape[0],), F32)]).reshape(1, LANES)
    in_specs = [pl.BlockSpec((tr, MLA_Q_RANK), lambda i: (i, c0)),
                pl.BlockSpec((tr, MLA_KV_RANK), lambda i: (i, c0 + 1)),
                row(LANES),
                pl.BlockSpec((tr, LANES), lambda i: (i % ncos, 0)),
                pl.BlockSpec((tr, LANES), lambda i: (i % ncos, 0)),
                const((1, MLA_Q_RANK)), const((1, MLA_KV_RANK)), const((1, hq)), const((1, LANES)), const((1, LANES)),
                const((MLA_Q_RANK, hq)), const((MLA_Q_RANK, hr))]
    args = [proj, proj, kpe_raw, cos, sin, g_cq.reshape(1, -1), g_ckv.reshape(1, -1), g_qn_row.reshape(1, hq),
            jnp.tile(g_qr, LANES // MLA_ROPE).reshape(1, LANES), pad_row(g_kr), wq_nope, wq_rope]
    out_specs = [row(hq), row(hr), row(MLA_KV_RANK), row(LANES)]
    out_shape = [jax.ShapeDtypeStruct((m, hq), BF16), jax.ShapeDtypeStruct((m, hr), BF16),
                 jax.ShapeDtypeStruct((m, MLA_KV_RANK), F32), jax.ShapeDtypeStruct((m, LANES), F32)]
    if with_kv:
        g_kn, w_uk, w_uv = kv_weights
        in_specs += [const((1, MLA_NOPE)), const((MLA_KV_RANK, hq)), const((MLA_KV_RANK, MLA_HEADS * MLA_V))]
        args += [g_kn.reshape(1, -1), w_uk, w_uv]
        out_specs += [row(hq), row(MLA_HEADS * MLA_V), row(LANES)]
        out_shape += [jax.ShapeDtypeStruct((m, hq), BF16), jax.ShapeDtypeStruct((m, MLA_HEADS * MLA_V), BF16),
                      jax.ShapeDtypeStruct((m, LANES), BF16)]
    return pl.pallas_call(
        functools.partial(_mla_prep_kernel, with_kv=with_kv),
        grid=(m // tr,),
        in_specs=in_specs, out_specs=out_specs, out_shape=out_shape,
        compiler_params=_params(("parallel",)),
        name="mla_prep",
    )(*args)


def _softmax_update(s, v, m_ref, l_ref, acc_ref):
    p, alpha = _softmax_stats(s, m_ref, l_ref)
    _softmax_accumulate(p, alpha, v, acc_ref)


def _softmax_stats(s, m_ref, l_ref):
    m_old = m_ref[...]
    m_new = jnp.maximum(m_old, jnp.max(s, axis=-1, keepdims=True))
    alpha = jnp.exp(m_old - m_new)
    p = jnp.exp(s - m_new[:, :1])
    l_ref[...] = alpha * l_ref[...] + jnp.sum(p, axis=-1, keepdims=True)
    m_ref[...] = m_new
    return p.astype(BF16), alpha[:, :1]


def _softmax_accumulate(p, alpha, v, acc_ref):
    acc_ref[...] = alpha * acc_ref[...] + _dot(p, v)


def _mla_prompt_kernel(qn_ref, qp_ref, kn_ref, kp_ref, vv_ref, o_ref, m_s, l_s, acc_s, *, tq, nh):
    qi = pl.program_id(2)
    lane_head = lax.broadcasted_iota(jnp.int32, (tq, LANES), 1) // MLA_ROPE
    row = lax.broadcasted_iota(jnp.int32, (tq, tq), 0)
    col = lax.broadcasted_iota(jnp.int32, (tq, tq), 1)
    m_s[...] = jnp.full(m_s.shape, -jnp.inf, F32)
    l_s[...] = jnp.zeros(l_s.shape, F32)
    acc_s[...] = jnp.zeros(acc_s.shape, F32)

    def process(j, mask):
        ks = pl.ds(pl.multiple_of(j * tq, tq), tq)
        kp = kp_ref[ks, :]
        head_cols = [slice(hh * MLA_NOPE, (hh + 1) * MLA_NOPE) for hh in range(nh)]
        ss = []
        for hh, cs in enumerate(head_cols):
            qp = qp_ref[:, (hh // 2) * LANES:(hh // 2 + 1) * LANES]
            q2 = jnp.where(lane_head == hh % 2, qp, jnp.zeros((), qp.dtype))
            s = _dot_nt(qn_ref[:, cs], kn_ref[ks, cs]) + _dot_nt(q2, kp)
            ss.append(s if mask is None else jnp.where(mask, s, -jnp.inf))
        pa = [_softmax_stats(s, m_s.at[hh], l_s.at[hh]) for hh, s in enumerate(ss)]
        for hh, cs in enumerate(head_cols):
            _softmax_accumulate(pa[hh][0], pa[hh][1], vv_ref[ks, cs], acc_s.at[hh])

    def body(j, _):
        process(j, None)
        return 0

    lax.fori_loop(0, qi, body, 0)
    process(qi, col <= row)
    for hh in range(nh):
        cs = slice(hh * MLA_NOPE, (hh + 1) * MLA_NOPE)
        o_ref[:, cs] = (acc_s[hh] / l_s[hh][:, :1]).astype(o_ref.dtype)


def mla_prompt(qn, qp, kn, kp2, vv, batch, seq, tq=256, nh=4):
    tq = min(tq, seq)
    nq = seq // tq
    hw = nh * MLA_NOPE
    rw = nh * MLA_ROPE
    return pl.pallas_call(
        functools.partial(_mla_prompt_kernel, tq=tq, nh=nh),
        grid=(batch, MLA_HEADS // nh, nq),
        in_specs=[pl.BlockSpec((tq, hw), lambda b, h, i: (b * nq + i, h)),
                  pl.BlockSpec((tq, rw), lambda b, h, i: (b * nq + i, h)),
                  pl.BlockSpec((seq, hw), lambda b, h, i: (b, h)),
                  pl.BlockSpec((seq, LANES), lambda b, h, i: (b, 0)),
                  pl.BlockSpec((seq, hw), lambda b, h, i: (b, h))],
        out_specs=pl.BlockSpec((tq, hw), lambda b, h, i: (b * nq + i, h)),
        out_shape=jax.ShapeDtypeStruct((batch * seq, MLA_HEADS * MLA_V), BF16),
        scratch_shapes=[pltpu.VMEM((nh, tq, LANES), F32), pltpu.VMEM((nh, tq, LANES), F32),
                        pltpu.VMEM((nh, tq, MLA_V), F32)],
        compiler_params=_params(("parallel", "parallel", "arbitrary")),
        name="mla_prompt",
    )(qn, qp, kn, kp2, vv)


def _head_matmul_kernel(a_ref, w_ref, o_ref):
    a = a_ref[...]
    a = a.reshape(-1, a.shape[-1]).astype(BF16)
    o_ref[...] = _dot(a, w_ref[0]).reshape(o_ref.shape).astype(o_ref.dtype)


def absorb_queries(qg, w_uk_t, nseq, t):
    return pl.pallas_call(
        _head_matmul_kernel,
        grid=(MLA_HEADS,),
        in_specs=[pl.BlockSpec((nseq * t, MLA_NOPE), lambda h: (0, h)),
                  pl.BlockSpec((1, MLA_NOPE, MLA_KV_RANK), lambda h: (h, 0, 0))],
        out_specs=pl.BlockSpec((nseq, None, t, MLA_KV_RANK), lambda h: (0, h, 0, 0)),
        out_shape=jax.ShapeDtypeStruct((nseq, MLA_HEADS, t, MLA_KV_RANK), F32),
        compiler_params=_params(("parallel",)),
        name="absorb_queries",
    )(qg, w_uk_t)


def project_latent_out(o_lat, w_uv_h, nseq, t):
    return pl.pallas_call(
        _head_matmul_kernel,
        grid=(MLA_HEADS,),
        in_specs=[pl.BlockSpec((nseq, None, t, MLA_KV_RANK), lambda h: (0, h, 0, 0)),
                  pl.BlockSpec((1, MLA_KV_RANK, MLA_V), lambda h: (h, 0, 0))],
        out_specs=pl.BlockSpec((nseq * t, MLA_V), lambda h: (0, h)),
        out_shape=jax.ShapeDtypeStruct((nseq * t, MLA_HEADS * MLA_V), BF16),
        compiler_params=_params(("parallel",)),
        name="project_latent_out",
    )(o_lat, w_uv_h)


MLA_BLOCK_PAGES = 4


def _mla_sample_kernel(pt_ref, qa_ref, qp_ref, cn_ref, kn_ref, wuk_ref, *refs, pages_per_step, t):
    c_refs = refs[:pages_per_step]
    k_refs = refs[pages_per_step:2 * pages_per_step]
    o_ref, lhs, m_s, l_s, acc_s, cpad, kpad = refs[2 * pages_per_step:]
    b = pl.program_id(0)
    p = pl.program_id(1)
    nk = MLA_HEADS * MLA_NOPE
    rows = MLA_HEADS * t

    @pl.when(jnp.logical_and(b == 0, p == 0))
    def _():
        lhs[0:nk, :] = wuk_ref[...]

    def front(c):
        keys = c.shape[0]
        kt = _dot_nt(lhs[...], c)
        k2 = kt[0:nk, :]
        k2 = (k2 * k2).reshape(MLA_HEADS, MLA_NOPE // 8, 8, keys).sum(axis=1)
        for sh in (4, 2, 1):
            k2 = k2 + pltpu.roll(k2, sh, 1)
        rs = lax.rsqrt(k2 * (1.0 / MLA_NOPE) + EPS).reshape(rows, keys)
        return kt[nk:nk + rows, :] * rs

    def back(s, c, kpe_t, mask):
        s = s + _dot(qp_ref[0].astype(BF16), kpe_t)
        if mask is not None:
            s = jnp.where(mask, s, -jnp.inf)
        _softmax_update(s, c, m_s, l_s, acc_s)

    @pl.when(p == 0)
    def _():
        lhs[nk:nk + rows, :] = qa_ref[0].reshape(rows, MLA_KV_RANK).astype(BF16)
        m_s[...] = jnp.full(m_s.shape, -jnp.inf, F32)
        l_s[...] = jnp.zeros(l_s.shape, F32)
        acc_s[...] = jnp.zeros(acc_s.shape, F32)
        cpad[...] = jnp.zeros(cpad.shape, F32)
        kpad[...] = jnp.zeros(kpad.shape, F32)
        cpad[0:t, :] = cn_ref[...]
        kpad[0:t, :] = kn_ref[...]
        row_t = lax.broadcasted_iota(jnp.int32, (rows, LANES), 0) % t
        col = lax.broadcasted_iota(jnp.int32, (rows, LANES), 1)
        cn = cpad[...].astype(BF16)
        back(front(cn), cn, kpad[...].T[0:MLA_ROPE, :].astype(BF16), col <= row_t)

    bp = min(MLA_BLOCK_PAGES, pages_per_step)
    latent = lambda i: jnp.concatenate([c_refs[i + k][...] for k in range(bp)], axis=0).astype(BF16)
    rotary = lambda i: jnp.concatenate([k_refs[i + k][...] for k in range(bp)], axis=1).astype(BF16)
    starts = list(range(0, pages_per_step, bp))
    pending = front(latent(starts[0]))
    for n, i in enumerate(starts):
        nxt = front(latent(starts[n + 1])) if n + 1 < len(starts) else None
        back(pending, latent(i), rotary(i), None)
        pending = nxt

    @pl.when(p == pl.num_programs(1) - 1)
    def _():
        o_ref[0] = (acc_s[...] / l_s[:, :1]).reshape(o_ref.shape[1:])


def mla_sample(q_abs, q_pe, ckv_new, kpe_new, w_uk_t2, cache_ckv, cache_kpe, layer, page_table, t, pages_per_step=16):
    nseq, n_pages = page_table.shape
    pps = min(pages_per_step, n_pages)
    nsteps = n_pages // pps
    rows = MLA_HEADS * t
    nk = MLA_HEADS * MLA_NOPE
    cache_kpe_t = jnp.swapaxes(cache_kpe, 2, 3)

    def page_spec(shape, i):
        return pl.BlockSpec((None, None) + shape, lambda b, p, pt: (layer, pt[b, p * pps + i], 0, 0))

    grid_spec = pltpu.PrefetchScalarGridSpec(
        num_scalar_prefetch=1,
        grid=(nseq, nsteps),
        in_specs=[pl.BlockSpec((1, MLA_HEADS, t, MLA_KV_RANK), lambda b, p, pt: (b, 0, 0, 0)),
                  pl.BlockSpec((1, rows, MLA_ROPE), lambda b, p, pt: (b, 0, 0)),
                  pl.BlockSpec((t, MLA_KV_RANK), lambda b, p, pt: (b, 0)),
                  pl.BlockSpec((t, LANES), lambda b, p, pt: (b, 0)),
                  pl.BlockSpec((nk, MLA_KV_RANK), lambda b, p, pt: (0, 0))]
                 + [page_spec((PAGE_SIZE, MLA_KV_RANK), i) for i in range(pps)]
                 + [page_spec((MLA_ROPE, PAGE_SIZE), i) for i in range(pps)],
        out_specs=pl.BlockSpec((1, MLA_HEADS, t, MLA_KV_RANK), lambda b, p, pt: (b, 0, 0, 0)),
        scratch_shapes=[pltpu.VMEM((nk + rows, MLA_KV_RANK), BF16),
                        pltpu.VMEM((rows, LANES), F32), pltpu.VMEM((rows, LANES), F32),
                        pltpu.VMEM((rows, MLA_KV_RANK), F32),
                        pltpu.VMEM((PAGE_SIZE, MLA_KV_RANK), F32), pltpu.VMEM((PAGE_SIZE, LANES), F32)],
    )
    return pl.pallas_call(
        functools.partial(_mla_sample_kernel, pages_per_step=pps, t=t),
        grid_spec=grid_spec,
        out_shape=jax.ShapeDtypeStruct((nseq, MLA_HEADS, t, MLA_KV_RANK), F32),
        compiler_params=_params(("arbitrary", "arbitrary")),
        name="mla_sample",
    )(page_table, q_abs, q_pe, ckv_new, kpe_new, w_uk_t2, *([cache_ckv] * pps), *([cache_kpe_t] * pps))


def _gelu(x):
    return 0.5 * x * (1.0 + jnp.tanh(0.7978845608028654 * (x + 0.044715 * (x * x * x))))


def _cmlp_kernel(u_ref, v_ref, gv_ref, bv_ref, w_ref, bs_ref, o_ref, *rest, seg):
    r = u_ref.shape[0]
    u = _gelu(u_ref[...])
    v = _gelu(v_ref[...])
    mu = jnp.mean(v, axis=-1, keepdims=True)
    vc = v - mu
    var = jnp.mean(vc * vc, axis=-1, keepdims=True)
    vn = vc * lax.rsqrt(var + EPS) * gv_ref[...] + bv_ref[...]
    if rest:
        rest[0][...] = vn
    vb = vn.astype(BF16)
    row = lax.broadcasted_iota(jnp.int32, (r, r), 0)
    col = lax.broadcasted_iota(jnp.int32, (r, r), 1)
    keep = jnp.logical_and(col <= row, (row // seg) == (col // seg))
    gw = CMLP_DIM // CMLP_GROUPS
    for g in range(CMLP_GROUPS):
        cs = slice(g * gw, (g + 1) * gw)
        w = jnp.where(keep, w_ref[g], 0.0).astype(BF16)
        mixed = _dot(w, vb[:, cs]) + bs_ref[:, g:g + 1]
        o_ref[:, cs] = (u[:, cs] * mixed).astype(o_ref.dtype)


def chunk_mlp(proj, g_v, b_v, w_mix, b_mix, seg, emit_v):
    m = proj.shape[0]
    d = CMLP_DIM
    const2 = lambda shape: pl.BlockSpec(shape, lambda i: (0, 0))
    out_specs = [pl.BlockSpec((ROW_TILE, d), lambda i: (i, 0))]
    out_shape = [jax.ShapeDtypeStruct((m, d), BF16)]
    if emit_v:
        out_specs.append(pl.BlockSpec((ROW_TILE, d), lambda i: (i, 0)))
        out_shape.append(jax.ShapeDtypeStruct((m, d), F32))
    return pl.pallas_call(
        functools.partial(_cmlp_kernel, seg=seg),
        grid=(m // ROW_TILE,),
        in_specs=[pl.BlockSpec((ROW_TILE, d), lambda i: (i, 0)),
                  pl.BlockSpec((ROW_TILE, d), lambda i: (i, 1)),
                  const2((1, d)), const2((1, d)),
                  pl.BlockSpec((CMLP_GROUPS, ROW_TILE, ROW_TILE), lambda i: (0, 0, 0)),
                  const2((ROW_TILE, CMLP_GROUPS))],
        out_specs=out_specs, out_shape=out_shape,
        compiler_params=_params(("parallel",)),
        name="chunk_mlp",
    )(proj, proj, g_v.reshape(1, d), b_v.reshape(1, d), w_mix, b_mix)


def _rope_tables(pos):
    half = MLA_ROPE // 2
    inv = ROPE_THETA ** (-jnp.arange(half, dtype=F32) / half)
    ang = pos.astype(F32)[:, None] * inv[None, :]
    reps = LANES // half
    return jnp.tile(jnp.cos(ang), (1, reps)), jnp.tile(jnp.sin(ang), (1, reps))


def _layer_c(xp, xs, batch, seq, nseq, t, cache_ckv, cache_kpe, layer, page_table, g_mix, g_ffn, w_in, g_cq, g_ckv,
             w_uq, g_qn, g_qr, g_kn, g_kr, w_uk, w_uv, g_v, b_v, w_s, b_s, w_out, w1, w2):
    past = page_table.shape[1] * PAGE_SIZE
    w_main = jnp.concatenate([w_in[:, C_KPE:], w_in[:, :C_CKV]], axis=1).astype(BF16)
    w_kpe = jnp.pad(w_in[:, C_CKV:C_KPE], ((0, 0), (0, LANES - MLA_ROPE))).astype(BF16)
    wq = w_uq.reshape(MLA_Q_RANK, MLA_HEADS, MLA_NOPE + MLA_ROPE)
    wq_nope = wq[:, :, :MLA_NOPE].reshape(MLA_Q_RANK, -1).astype(BF16)
    wq_rope = wq[:, :, MLA_NOPE:].reshape(MLA_Q_RANK, -1).astype(BF16)
    w_uk2 = w_uk.reshape(MLA_KV_RANK, -1).astype(BF16)
    w_uv2 = w_uv.reshape(MLA_KV_RANK, -1).astype(BF16)
    w_uk_t = w_uk.transpose(1, 2, 0).astype(BF16)
    w_uv_h = w_uv.transpose(1, 0, 2).astype(BF16)
    w_out = w_out.astype(BF16)
    w1 = w1.astype(BF16)
    w2 = w2.astype(BF16)
    ones_row = jnp.ones((MLA_HEADS * MLA_NOPE,), F32)

    pp = norm_matmul(xp, g_mix, w_main)
    kp = norm_matmul(xp, g_mix, w_kpe)
    cos_p, sin_p = _rope_tables(jnp.arange(seq))
    qn, qp, ckv_p, kpe_p, kn, vv, kp2 = mla_prep(pp, kp, cos_p, sin_p, g_cq, g_ckv, jnp.tile(g_qn, MLA_HEADS) * ones_row,
                                                 g_qr, g_kr, wq_nope, wq_rope, (g_kn, w_uk2, w_uv2))
    o_mla_p = mla_prompt(qn, qp, kn, kp2, vv, batch, seq)
    o_cm_p = chunk_mlp(pp, g_v, b_v, w_s, b_s.T, CMLP_CHUNK, False)[0]
    xp = out_proj(o_mla_p, o_cm_p, w_out, xp)
    xp = ffn(xp, g_ffn, w1, w2)
    ckv_p = ckv_p.reshape(batch, seq, MLA_KV_RANK)
    kpe_p = kpe_p[:, :MLA_ROPE].reshape(batch, seq, MLA_ROPE)

    ps = norm_matmul(xs, g_mix, w_main)
    ks = norm_matmul(xs, g_mix, w_kpe)
    rows_s = nseq * t
    tr = min(MLA_PREP_ROWS, rows_s)
    cos_s, sin_s = _rope_tables(past + (jnp.arange(tr) % t))
    qg, qr, ckv_s, kpe_s = mla_prep(ps, ks, cos_s, sin_s, g_cq, g_ckv, jnp.tile(g_qn * g_kn, MLA_HEADS), g_qr, g_kr,
                                    wq_nope, wq_rope)
    q_abs = absorb_queries(qg, w_uk_t, nseq, t)
    q_pe = qr.reshape(nseq, t, MLA_HEADS, MLA_ROPE).transpose(0, 2, 1, 3).reshape(nseq, MLA_HEADS * t, MLA_ROPE)
    o_lat = mla_sample(q_abs, q_pe.astype(F32), ckv_s, kpe_s, w_uk2.T, cache_ckv, cache_kpe, layer, page_table, t)
    o_mla_s = project_latent_out(o_lat, w_uv_h, nseq, t)
    reps = ROW_TILE // t
    w_mix_s = jnp.tile(w_s[:, :t, :t], (1, reps, reps))
    b_mix_s = jnp.tile(b_s[:, :t].T, (reps, 1))
    o_cm_s, cv_s = chunk_mlp(ps, g_v, b_v, w_mix_s, b_mix_s, t, True)
    xs = out_proj(o_mla_s, o_cm_s, w_out, xs)
    xs = ffn(xs, g_ffn, w1, w2)
    ckv_s = ckv_s.reshape(nseq, t, MLA_KV_RANK)
    kpe_s = kpe_s[:, :MLA_ROPE].reshape(nseq, t, MLA_ROPE)
    return xp, xs, (ckv_p, kpe_p), (ckv_s, kpe_s, cv_s.reshape(nseq, t, CMLP_DIM))


def kernel(x_prompt, x_sample, cache_sb_k, cache_sb_v, cache_mla_ckv, cache_mla_kpe, state_ssm, state_conv, page_table, norm_mix, norm_ffn, w_in_a, w_conv, b_conv, dt_bias, a_log, d_skip, g_ssd, w_out_a, w_in_c, g_cq, g_ckv, w_uq, g_qn, g_qr, g_kn, g_kr, w_uk, w_uv, g_v, b_v, w_s, b_s, w_out_c, w_ffn1, w_ffn2):
    batch, seq, dm = x_prompt.shape
    nseq, t, _ = x_sample.shape
    depth = norm_mix.shape[0]
    xp = x_prompt.reshape(batch * seq, dm)
    xs = x_sample.reshape(nseq * t, dm)
    pa, sa, pc, sc = [], [], [], []
    for l in range(depth):
        i = l // 2
        if l % 2 == 0:
            xp, xs, op, os_ = _layer_a(xp, xs, batch, seq, nseq, t, cache_sb_k, cache_sb_v, i, state_ssm[i], state_conv[i],
                                       page_table, norm_mix[l], norm_ffn[l], w_in_a[i], w_conv[i], b_conv[i], dt_bias[i],
                                       a_log[i], d_skip[i], g_ssd[i], w_out_a[i], w_ffn1[l], w_ffn2[l])
            pa.append(op)
            sa.append(os_)
        else:
            xp, xs, op, os_ = _layer_c(xp, xs, batch, seq, nseq, t, cache_mla_ckv, cache_mla_kpe, i, page_table,
                                       norm_mix[l], norm_ffn[l], w_in_c[i], g_cq[i], g_ckv[i], w_uq[i], g_qn[i], g_qr[i],
                                       g_kn[i], g_kr[i], w_uk[i], w_uv[i], g_v[i], b_v[i], w_s[i], b_s[i], w_out_c[i],
                                       w_ffn1[l], w_ffn2[l])
            pc.append(op)
            sc.append(os_)
    stack = lambda items, k: jnp.stack([it[k] for it in items])
    return (xp.reshape(batch, seq, dm), xs.reshape(nseq, t, dm),
            stack(pa, 0), stack(pa, 1), stack(pa, 2), stack(pa, 3), stack(pc, 0), stack(pc, 1),
            stack(sa, 0), stack(sa, 1), stack(sa, 2), stack(sa, 3), stack(sc, 0), stack(sc, 1), stack(sc, 2))
```

```python
import functools

import jax
import jax.numpy as jnp
from jax import lax
from jax.experimental import pallas as pl
from jax.experimental.pallas import tpu as pltpu

F32 = jnp.float32
BF16 = jnp.bfloat16

D_MODEL = 2048
PAGE_SIZE = 128
SSD_D_INNER = D_MODEL
SSD_HEAD_DIM = 64
SSD_HEADS = SSD_D_INNER // SSD_HEAD_DIM
SSD_GROUPS = 8
SSD_STATE = 128
SSD_CONV = 4
SSD_CHUNK = 128
SSD_CONV_DIM = SSD_D_INNER + 2 * SSD_GROUPS * SSD_STATE
SB_HEAD_DIM = 128
SB_HEADS = D_MODEL // SB_HEAD_DIM
SB_KV_HEADS = 4
SB_Q_PER_KV = SB_HEADS // SB_KV_HEADS
SB_SCALE = SB_HEAD_DIM ** -0.5
MLA_HEADS = 16
MLA_Q_RANK = 512
MLA_KV_RANK = 512
MLA_NOPE = 128
MLA_ROPE = 64
MLA_V = 128
MLA_SCALE = (MLA_NOPE + MLA_ROPE) ** -0.5
ROPE_THETA = 10000.0
CMLP_DIM = D_MODEL
CMLP_GROUPS = 8
CMLP_CHUNK = 128
FFN_DIM = 4 * D_MODEL
EPS = 1e-6

A_Z = SSD_D_INNER
A_XBC = A_Z + SSD_CONV_DIM
A_DT = A_XBC + SSD_HEADS
A_Q = A_DT + SB_HEADS * SB_HEAD_DIM
A_K = A_Q + SB_KV_HEADS * SB_HEAD_DIM
C_CQ = MLA_Q_RANK
C_CKV = C_CQ + MLA_KV_RANK
C_KPE = C_CKV + MLA_ROPE
C_U = C_KPE + CMLP_DIM

LANES = 128
ROW_TILE = 128
VMEM_LIMIT = 56 * 1024 * 1024


def _params(semantics, vmem=VMEM_LIMIT, flags=None):
    return pltpu.CompilerParams(dimension_semantics=semantics, vmem_limit_bytes=vmem, flags=flags)


def _split3(x):
    hi = x.astype(BF16)
    r1 = x - hi.astype(F32)
    mid = r1.astype(BF16)
    lo = (r1 - mid.astype(F32)).astype(BF16)
    return hi, mid, lo


def _dot(a, b):
    return jnp.dot(a, b, preferred_element_type=F32)


def _dot_nt(a, b):
    return lax.dot_general(a, b, (((1,), (1,)), ((), ())), preferred_element_type=F32)


def _dot01_right(x, m01):
    hi, mid, lo = _split3(x)
    return _dot(hi, m01) + _dot(mid, m01) + _dot(lo, m01)


def _dot01_left(m01, x):
    hi, mid, lo = _split3(x)
    return _dot(m01, hi) + _dot(m01, mid) + _dot(m01, lo)


def _softplus(x):
    return jnp.maximum(x, 0.0) + jnp.log1p(jnp.exp(-jnp.abs(x)))


def _silu(x):
    return x * jax.nn.sigmoid(x)


def _rms(x, g):
    ms = jnp.mean(x * x, axis=-1, keepdims=True)
    return x * lax.rsqrt(ms + EPS) * g


def _norm_matmul_kernel(x_ref, g_ref, w_ref, o_ref, h_ref):
    @pl.when(pl.program_id(1) == 0)
    def _():
        h_ref[...] = _rms(x_ref[...], g_ref[...]).astype(BF16)

    o_ref[...] = _dot(h_ref[...], w_ref[...])


def norm_matmul(x, g, w, tm=1024, tn=512):
    m, k = x.shape
    n = w.shape[1]
    tm, tn = min(tm, m), min(tn, n)
    return pl.pallas_call(
        _norm_matmul_kernel,
        grid=(m // tm, n // tn),
        in_specs=[pl.BlockSpec((tm, k), lambda i, j: (i, 0)),
                  pl.BlockSpec((1, k), lambda i, j: (0, 0)),
                  pl.BlockSpec((k, tn), lambda i, j: (0, j))],
        out_specs=pl.BlockSpec((tm, tn), lambda i, j: (i, j)),
        out_shape=jax.ShapeDtypeStruct((m, n), F32),
        scratch_shapes=[pltpu.VMEM((tm, k), BF16)],
        compiler_params=_params(("parallel", "arbitrary")),
        name="norm_matmul",
    )(x, g.reshape(1, k), w)


def _out_proj_kernel(a_ref, b_ref, wa_ref, wb_ref, r_ref, o_ref):
    o_ref[...] = r_ref[...] + (_dot(a_ref[...], wa_ref[...]) + _dot(b_ref[...], wb_ref[...]))


def out_proj(a, b, w, res, tm=1024, tn=512):
    m, kh = a.shape
    n = w.shape[1]
    tm, tn = min(tm, m), min(tn, n)
    return pl.pallas_call(
        _out_proj_kernel,
        grid=(m // tm, n // tn),
        in_specs=[pl.BlockSpec((tm, kh), lambda i, j: (i, 0)),
                  pl.BlockSpec((tm, kh), lambda i, j: (i, 0)),
                  pl.BlockSpec((kh, tn), lambda i, j: (0, j)),
                  pl.BlockSpec((kh, tn), lambda i, j: (1, j)),
                  pl.BlockSpec((tm, tn), lambda i, j: (i, j))],
        out_specs=pl.BlockSpec((tm, tn), lambda i, j: (i, j)),
        out_shape=jax.ShapeDtypeStruct((m, n), F32),
        compiler_params=_params(("parallel", "arbitrary")),
        name="out_proj",
    )(a, b, w, w, res)


def _ffn_kernel(x_ref, g_ref, w1_ref, w2_ref, o_ref, h_ref):
    @pl.when(pl.program_id(1) == 0)
    def _():
        x = x_ref[...]
        h_ref[...] = _rms(x, g_ref[...]).astype(BF16)
        o_ref[...] = x

    a = jnp.maximum(_dot(h_ref[...], w1_ref[...]), 0.0)
    o_ref[...] += _dot((a * a).astype(BF16), w2_ref[...])


def ffn(x, g, w1, w2, tm=1024, tf=512):
    m, d = x.shape
    f = w1.shape[1]
    tm, tf = min(tm, m), min(tf, f)
    return pl.pallas_call(
        _ffn_kernel,
        grid=(m // tm, f // tf),
        in_specs=[pl.BlockSpec((tm, d), lambda i, j: (i, 0)),
                  pl.BlockSpec((1, d), lambda i, j: (0, 0)),
                  pl.BlockSpec((d, tf), lambda i, j: (0, j)),
                  pl.BlockSpec((tf, d), lambda i, j: (j, 0))],
        out_specs=pl.BlockSpec((tm, d), lambda i, j: (i, 0)),
        out_shape=jax.ShapeDtypeStruct((m, d), F32),
        scratch_shapes=[pltpu.VMEM((tm, d), BF16)],
        compiler_params=_params(("parallel", "arbitrary")),
        name="ffn",
    )(x, g.reshape(1, d), w1, w2)


def _head_expand():
    h = lax.broadcasted_iota(jnp.int32, (LANES, SSD_D_INNER), 0)
    c = lax.broadcasted_iota(jnp.int32, (LANES, SSD_D_INNER), 1)
    return (c // SSD_HEAD_DIM == h).astype(BF16)


def _conv_silu(pad_ref, w_ref, b_ref, lead, rows):
    acc = b_ref[...] + w_ref[SSD_CONV - 1:SSD_CONV, :] * pad_ref[lead:lead + rows, :]
    for k in range(SSD_CONV - 1):
        s = lead - (SSD_CONV - 1) + k
        acc = acc + w_ref[k:k + 1, :] * pad_ref[s:s + rows, :]
    return _silu(acc)


def _ssd_intra(xs, bc, z_unused, dt_raw, dtb, alog, e01, same_block):
    r = xs.shape[0]
    row = lax.broadcasted_iota(jnp.int32, (r, r), 0)
    col = lax.broadcasted_iota(jnp.int32, (r, r), 1)
    causal = jnp.logical_and(col <= row, same_block)
    dt = _softplus(dt_raw + dtb)
    dta = dt * (-jnp.exp(alog))
    at = _dot01_left(causal.astype(BF16), dta)
    at_t = at.T
    at_x = _dot01_right(at, e01)
    dt_x = _dot01_right(dt, e01)
    xdt = xs * dt_x
    return dict(causal=causal, at=at, at_t=at_t, at_x=at_x, xdt=xdt)


def _ssd_diag_pair(q, pr, cb, at, at_t, causal, xdt_bf):
    lane = lax.broadcasted_iota(jnp.int32, (q, LANES), 1)
    ys = []
    for h in (2 * pr, 2 * pr + 1):
        d = at[:, h:h + 1] - at_t[h:h + 1, :]
        dec = jnp.exp(jnp.where(causal, d, -jnp.inf))
        ys.append(_dot((cb * dec).astype(BF16), xdt_bf))
    return jnp.where(lane < SSD_HEAD_DIM, ys[0], ys[1])


def _gated_group_norm(y, z, g_ref, o_ref):
    gw = SSD_D_INNER // SSD_GROUPS
    for g in range(SSD_GROUPS):
        cs = slice(g * gw, (g + 1) * gw)
        yg = y[:, cs] * _silu(z[:, cs])
        o_ref[:, cs] = _rms(yg, g_ref[:, cs]).astype(o_ref.dtype)


def _ssd_prompt_kernel(z_ref, xs_ref, bc_ref, dt_ref, wcx_ref, wcb_ref, bcx_ref, bcb_ref, dtb_ref,
                       alog_ref, dsk_ref, gss_ref, e_ref, y_ref, s_ref, padx, padb, st, ybuf):
    c = pl.program_id(1)
    q = ROW_TILE
    lead = 8
    gn = SSD_GROUPS * SSD_STATE

    @pl.when(c == 0)
    def _():
        padx[0:lead, :] = jnp.zeros((lead, SSD_D_INNER), F32)
        padb[0:lead, :] = jnp.zeros((lead, 2 * gn), F32)
        st[...] = jnp.zeros(st.shape, F32)

    padx[lead:lead + q, :] = xs_ref[...]
    padb[lead:lead + q, :] = bc_ref[...]
    xs = _conv_silu(padx, wcx_ref, bcx_ref, lead, q)
    bc = _conv_silu(padb, wcb_ref, bcb_ref, lead, q)
    padx[0:lead, :] = padx[q:q + lead, :]
    padb[0:lead, :] = padb[q:q + lead, :]

    e01 = e_ref[...]
    whole = jnp.full((q, q), True)
    r = _ssd_intra(xs, bc, None, dt_ref[...], dtb_ref[...], alog_ref[...], e01, whole)
    at, at_t, at_x, xdt, causal = r["at"], r["at_t"], r["at_x"], r["xdt"], r["causal"]
    eat = jnp.exp(at_x)
    last = at_x[q - 1:q, :]
    xw = (xdt * jnp.exp(last - at_x)).astype(BF16)
    cd = jnp.exp(last)
    xdt_bf = xdt.astype(BF16)

    for g in range(SSD_GROUPS):
        bm = bc[:, g * SSD_STATE:(g + 1) * SSD_STATE]
        cm = bc[:, gn + g * SSD_STATE:gn + (g + 1) * SSD_STATE].astype(BF16)
        cb = _dot_nt(cm, bm.astype(BF16))
        bm_t = bm.T.astype(BF16)
        for pr in (2 * g, 2 * g + 1):
            cs = slice(pr * LANES, (pr + 1) * LANES)
            yd = _ssd_diag_pair(q, pr, cb, at, at_t, causal, xdt_bf[:, cs])
            s_in = st[:, cs]
            yo = _dot(cm, s_in.astype(BF16)) * eat[:, cs]
            st[:, cs] = s_in * cd[:, cs] + _dot(bm_t, xw[:, cs])
            ybuf[:, cs] = yd + yo

    y = ybuf[...] + dsk_ref[...] * xs
    _gated_group_norm(y, z_ref[...], gss_ref, y_ref)

    @pl.when(c == pl.num_programs(1) - 1)
    def _():
        for pr in range(SSD_D_INNER // LANES):
            cs = slice(pr * LANES, (pr + 1) * LANES)
            s_ref[0, cs, :] = st[:, cs].T


def ssd_prompt(proj, dtp, batch, seq, w_conv, b_conv, dt_bias, a_log, d_skip, g_ssd):
    nc = seq // ROW_TILE
    gn = SSD_GROUPS * SSD_STATE
    d = SSD_D_INNER
    row = lambda b, c: b * nc + c
    const = lambda shape: pl.BlockSpec(shape, lambda b, c: (0, 0))
    pad128 = lambda v: jnp.pad(v, (0, LANES - v.shape[0])).reshape(1, LANES)
    return pl.pallas_call(
        _ssd_prompt_kernel,
        grid=(batch, nc),
        in_specs=[pl.BlockSpec((ROW_TILE, d), lambda b, c: (row(b, c), 0)),
                  pl.BlockSpec((ROW_TILE, d), lambda b, c: (row(b, c), 1)),
                  pl.BlockSpec((ROW_TILE, 2 * gn), lambda b, c: (row(b, c), 2)),
                  pl.BlockSpec((ROW_TILE, LANES), lambda b, c: (row(b, c), 0)),
                  const((SSD_CONV, d)), const((SSD_CONV, 2 * gn)), const((1, d)), const((1, 2 * gn)),
                  const((1, LANES)), const((1, LANES)), const((1, d)), const((1, d)),
                  const((LANES, d))],
        out_specs=[pl.BlockSpec((ROW_TILE, d), lambda b, c: (row(b, c), 0)),
                   pl.BlockSpec((1, d, SSD_STATE), lambda b, c: (b, 0, 0))],
        out_shape=[jax.ShapeDtypeStruct((batch * seq, d), BF16),
                   jax.ShapeDtypeStruct((batch, d, SSD_STATE), F32)],
        scratch_shapes=[pltpu.VMEM((ROW_TILE + 8, d), F32), pltpu.VMEM((ROW_TILE + 8, 2 * gn), F32),
                        pltpu.VMEM((SSD_STATE, d), F32), pltpu.VMEM((ROW_TILE, d), F32)],
        compiler_params=_params(("parallel", "arbitrary")),
        name="ssd_prompt",
    )(proj, proj, proj, dtp, w_conv[:, :d], w_conv[:, d:], b_conv[:d].reshape(1, d),
      b_conv[d:].reshape(1, 2 * gn), pad128(dt_bias), pad128(a_log),
      jnp.repeat(d_skip, SSD_HEAD_DIM).reshape(1, d), g_ssd.reshape(1, d), _head_expand())


def _ssd_sample_kernel(z_ref, xs_ref, bc_ref, dt_ref, cvx_ref, cvb_ref, s0_ref, wcx_ref, wcb_ref,
                       bcx_ref, bcb_ref, dtb_ref, alog_ref, dsk_ref, gss_ref, e_ref, et_ref,
                       y_ref, cox_ref, cob_ref, s_ref,
                       padx, padb, ybuf, yoff, eat_s, cm_s, bm_s, xwt_s, att_s):
    j = pl.program_id(1)
    r = ROW_TILE
    t = xs_ref.shape[0] // cvx_ref.shape[0]
    nseq = r // t
    lead = 8
    gn = SSD_GROUPS * SSD_STATE
    d = SSD_D_INNER
    hist = SSD_CONV - 1

    @pl.when(j == 0)
    def _():
        def conv(pad, raw_ref, cv_ref, w_ref, b_ref, co_ref):
            width = raw_ref.shape[1]
            pad[:, lead - hist:lead, :] = cv_ref[...]
            pad[:, lead:lead + t, :] = raw_ref[...].reshape(nseq, t, width)
            acc = b_ref[...] + w_ref[hist:hist + 1, :] * pad[:, lead:lead + t, :]
            for k in range(hist):
                s = lead - hist + k
                acc = acc + w_ref[k:k + 1, :] * pad[:, s:s + t, :]
            co_ref[...] = pad[:, lead + t - hist:lead + t, :]
            return _silu(acc).reshape(r, width)

        xs = conv(padx, xs_ref, cvx_ref, wcx_ref, bcx_ref, cox_ref)
        bc = conv(padb, bc_ref, cvb_ref, wcb_ref, bcb_ref, cob_ref)
        e01 = e_ref[...]
        row = lax.broadcasted_iota(jnp.int32, (r, r), 0)
        col = lax.broadcasted_iota(jnp.int32, (r, r), 1)
        same = (row // t) == (col // t)
        q = _ssd_intra(xs, bc, None, dt_ref[...], dtb_ref[...], alog_ref[...], e01, same)
        at, at_t, at_x, xdt, causal = q["at"], q["at_t"], q["at_x"], q["xdt"], q["causal"]
        eat_s[...] = jnp.exp(at_x)
        last_sel = jnp.logical_and(same, (col % t) == t - 1).astype(BF16)
        last = _dot01_left(last_sel, at_x)
        xw = xdt * jnp.exp(last - at_x)
        xdt_bf = xdt.astype(BF16)
        att_s[...] = _dot01_left(et_ref[...], at_t)
        for g in range(SSD_GROUPS):
            bm = bc[:, g * SSD_STATE:(g + 1) * SSD_STATE].astype(BF16)
            cm = bc[:, gn + g * SSD_STATE:gn + (g + 1) * SSD_STATE].astype(BF16)
            bm_s[:, g * SSD_STATE:(g + 1) * SSD_STATE] = bm
            cm_s[:, g * SSD_STATE:(g + 1) * SSD_STATE] = cm
            cb = _dot_nt(cm, bm)
            for pr in (2 * g, 2 * g + 1):
                cs = slice(pr * LANES, (pr + 1) * LANES)
                ybuf[:, cs] = _ssd_diag_pair(r, pr, cb, at, at_t, causal, xdt_bf[:, cs])
                xwt_s[cs, :] = xw[:, cs].T.astype(BF16)
        ybuf[...] = ybuf[...] + dsk_ref[...] * xs
        yoff[...] = jnp.zeros(yoff.shape, F32)

    rowseq = lax.broadcasted_iota(jnp.int32, (r, LANES), 0) // t
    laneseq = lax.broadcasted_iota(jnp.int32, (LANES, r), 1) // t
    lastlane = lax.broadcasted_iota(jnp.int32, (LANES, r), 1) == j * t + t - 1
    gw = d // SSD_GROUPS
    for g in range(SSD_GROUPS):
        rs = slice(g * gw, (g + 1) * gw)
        gs = slice(g * SSD_STATE, (g + 1) * SSD_STATE)
        s0 = s0_ref[0, rs, :]
        yo = _dot_nt(cm_s[:, gs], s0.astype(BF16))
        for half in range(gw // LANES):
            cs = slice(g * gw + half * LANES, g * gw + (half + 1) * LANES)
            yoff[:, cs] += jnp.where(rowseq == j, yo[:, half * LANES:(half + 1) * LANES], 0.0)
        sel = jnp.concatenate([laneseq, laneseq], axis=0) == j
        sc = _dot(jnp.where(sel, xwt_s[rs, :], jnp.zeros((), BF16)), bm_s[:, gs])
        lastc = jnp.concatenate([lastlane, lastlane], axis=0)
        cdc = jnp.exp(jnp.sum(jnp.where(lastc, att_s[rs, :], 0.0), axis=1, keepdims=True))
        s_ref[0, rs, :] = s0 * cdc + sc

    @pl.when(j == pl.num_programs(1) - 1)
    def _():
        y = ybuf[...] + yoff[...] * eat_s[...]
        _gated_group_norm(y, z_ref[...], gss_ref, y_ref)


def ssd_sample(proj, dtp, nseq, t, state_conv, state_ssm, w_conv, b_conv, dt_bias, a_log, d_skip, g_ssd):
    spb = ROW_TILE // t
    nb = nseq // spb
    gn = SSD_GROUPS * SSD_STATE
    d = SSD_D_INNER
    hist = SSD_CONV - 1
    const = lambda shape: pl.BlockSpec(shape, lambda i, j: (0, 0))
    pad128 = lambda v: jnp.pad(v, (0, LANES - v.shape[0])).reshape(1, LANES)
    e01 = _head_expand()
    outs = pl.pallas_call(
        _ssd_sample_kernel,
        grid=(nb, spb),
        in_specs=[pl.BlockSpec((ROW_TILE, d), lambda i, j: (i, 0)),
                  pl.BlockSpec((ROW_TILE, d), lambda i, j: (i, 1)),
                  pl.BlockSpec((ROW_TILE, 2 * gn), lambda i, j: (i, 2)),
                  pl.BlockSpec((ROW_TILE, LANES), lambda i, j: (i, 0)),
                  pl.BlockSpec((spb, hist, d), lambda i, j: (i, 0, 0)),
                  pl.BlockSpec((spb, hist, 2 * gn), lambda i, j: (i, 0, 1)),
                  pl.BlockSpec((1, d, SSD_STATE), lambda i, j: (i * spb + j, 0, 0)),
                  const((SSD_CONV, d)), const((SSD_CONV, 2 * gn)), const((1, d)), const((1, 2 * gn)),
                  const((1, LANES)), const((1, LANES)), const((1, d)), const((1, d)),
                  const((LANES, d)), const((d, LANES))],
        out_specs=[pl.BlockSpec((ROW_TILE, d), lambda i, j: (i, 0)),
                   pl.BlockSpec((spb, hist, d), lambda i, j: (i, 0, 0)),
                   pl.BlockSpec((spb, hist, 2 * gn), lambda i, j: (i, 0, 0)),
                   pl.BlockSpec((1, d, SSD_STATE), lambda i, j: (i * spb + j, 0, 0))],
        out_shape=[jax.ShapeDtypeStruct((nseq * t, d), BF16),
                   jax.ShapeDtypeStruct((nseq, hist, d), F32),
                   jax.ShapeDtypeStruct((nseq, hist, 2 * gn), F32),
                   jax.ShapeDtypeStruct((nseq, d, SSD_STATE), F32)],
        scratch_shapes=[pltpu.VMEM((spb, 8 + t, d), F32), pltpu.VMEM((spb, 8 + t, 2 * gn), F32),
                        pltpu.VMEM((ROW_TILE, d), F32), pltpu.VMEM((ROW_TILE, d), F32),
                        pltpu.VMEM((ROW_TILE, d), F32),
                        pltpu.VMEM((ROW_TILE, gn), BF16), pltpu.VMEM((ROW_TILE, gn), BF16),
                        pltpu.VMEM((d, ROW_TILE), BF16), pltpu.VMEM((d, ROW_TILE), F32)],
        compiler_params=_params(("parallel", "arbitrary")),
        name="ssd_sample",
    )(proj, proj, proj, dtp, state_conv, state_conv, state_ssm,
      w_conv[:, :d], w_conv[:, d:], b_conv[:d].reshape(1, d), b_conv[d:].reshape(1, 2 * gn),
      pad128(dt_bias), pad128(a_log), jnp.repeat(d_skip, SSD_HEAD_DIM).reshape(1, d),
      g_ssd.reshape(1, d), e01, e01.T)
    y, cox, cob, s_new = outs
    return y, jnp.concatenate([cox, cob], axis=-1), s_new


def _later_matrix():
    j = lax.broadcasted_iota(jnp.int32, (2 * LANES, 2 * LANES), 0) % LANES
    s = lax.broadcasted_iota(jnp.int32, (2 * LANES, 2 * LANES), 1)
    return jnp.where(jnp.logical_or(j > s, s >= LANES), -1.0, 0.0).astype(BF16)


def _sb_weights(zs, vis, carry, later01):
    assert vis is None or len(zs) == 1
    rows = zs[0].shape[0]
    z = jnp.concatenate(zs, axis=0)
    sp = jnp.maximum(z, 0.0) + jnp.log(1.0 + jnp.exp(-jnp.abs(z)))
    ls = z - sp
    if vis is not None:
        sp = jnp.where(vis, sp, 0.0)
    hi = sp.astype(BF16)
    lo = (sp - hi.astype(F32)).astype(BF16)
    lc = _dot(jnp.concatenate([hi, lo], axis=1), later01)
    ws = []
    for i in range(len(zs)):
        rs = slice(i * rows, (i + 1) * rows)
        w = jnp.exp(ls[rs] + lc[rs, :LANES] + carry)
        ws.append(w if vis is None else jnp.where(vis, w, 0.0))
        carry = carry + lc[rs, LANES:]
    return ws, carry


def _sb_prompt_kernel(q_ref, k_ref, v_ref, o_ref, acc, carry):
    qi = pl.program_id(2)
    t = ROW_TILE
    g4 = SB_Q_PER_KV
    later01 = _later_matrix()
    q = q_ref[...]
    qs = jnp.concatenate([q[:, g * SB_HEAD_DIM:(g + 1) * SB_HEAD_DIM] for g in range(g4)], axis=0)
    qs = (qs * SB_SCALE).astype(BF16)
    rows = g4 * t
    chunks = 2
    hr = rows // chunks

    def kv_rows(j):
        return pl.ds(j * t, t) if isinstance(j, int) else pl.ds(pl.multiple_of(j * t, t), t)

    def step(js, vis):
        ks = [k_ref[kv_rows(j), :].astype(BF16) for j in js]
        vs = jnp.concatenate([v_ref[kv_rows(j), :].astype(BF16) for j in js], axis=0)
        zs = [[_dot_nt(qs[c * hr:(c + 1) * hr], k) for k in ks] for c in range(chunks)]
        wcat = []
        for c in range(chunks):
            rs = slice(c * hr, (c + 1) * hr)
            ws, cn = _sb_weights(zs[c], vis, carry[rs, :], later01)
            carry[rs, :] = cn
            wcat.append(jnp.concatenate([w.astype(BF16) for w in ws], axis=1))
        for c in range(chunks):
            acc[c * hr:(c + 1) * hr, :] += _dot(wcat[c], vs)

    acc[...] = jnp.zeros(acc.shape, F32)
    carry[...] = jnp.zeros(carry.shape, F32)
    row_t = lax.broadcasted_iota(jnp.int32, (hr, t), 0) % t
    col = lax.broadcasted_iota(jnp.int32, (hr, t), 1)
    step([qi], col < row_t)

    def body(i, _):
        j = qi - 1 - 2 * i
        step([j, j - 1], None)
        return 0

    lax.fori_loop(0, qi // 2, body, 0)

    @pl.when(qi % 2 == 1)
    def _():
        step([0], None)

    for g in range(g4):
        o_ref[:, g * SB_HEAD_DIM:(g + 1) * SB_HEAD_DIM] = acc[g * t:(g + 1) * t, :].astype(o_ref.dtype)


def sb_prompt(proj, batch, seq):
    nq = seq // ROW_TILE
    qw = SB_Q_PER_KV * SB_HEAD_DIM
    q0 = (A_XBC) // qw
    k0 = (A_XBC + SB_HEADS * SB_HEAD_DIM) // SB_HEAD_DIM
    v0 = k0 + SB_KV_HEADS
    return pl.pallas_call(
        _sb_prompt_kernel,
        grid=(batch, SB_KV_HEADS, nq),
        in_specs=[pl.BlockSpec((ROW_TILE, qw), lambda b, h, i: (b * nq + i, q0 + h)),
                  pl.BlockSpec((seq, SB_HEAD_DIM), lambda b, h, i: (b, k0 + h)),
                  pl.BlockSpec((seq, SB_HEAD_DIM), lambda b, h, i: (b, v0 + h))],
        out_specs=pl.BlockSpec((ROW_TILE, qw), lambda b, h, i: (b * nq + i, h)),
        out_shape=jax.ShapeDtypeStruct((batch * seq, SB_HEADS * SB_HEAD_DIM), BF16),
        scratch_shapes=[pltpu.VMEM((SB_Q_PER_KV * ROW_TILE, SB_HEAD_DIM), F32),
                        pltpu.VMEM((SB_Q_PER_KV * ROW_TILE, LANES), F32)],
        compiler_params=_params(("parallel", "parallel", "arbitrary")),
        name="sb_prompt",
    )(proj, proj, proj)


def _sb_sample_kernel(pt_ref, q_ref, kn_ref, vn_ref, *refs, pages_per_step, t):
    k_refs = refs[:pages_per_step]
    v_refs = refs[pages_per_step:2 * pages_per_step]
    o_ref, acc, carry, knp, vnp = refs[2 * pages_per_step:]
    p = pl.program_id(1)
    rows = q_ref.shape[1]
    rk = rows // SB_KV_HEADS
    later01 = _later_matrix()
    qs = (q_ref[0] * SB_SCALE).astype(BF16)

    heads = range(SB_KV_HEADS)
    hrow = lambda h: slice(h * rk, (h + 1) * rk)

    def blocks(zs, vfns, vis):
        ws, c = _sb_weights(zs, vis, carry[...], later01)
        carry[...] = c
        ws = [w.astype(BF16) for w in ws]
        outs = []
        for h in range(0, SB_KV_HEADS, 2):
            wh = jnp.concatenate([w[h * rk:(h + 2) * rk] for w in ws], axis=1)
            vh = jnp.concatenate([jnp.concatenate([vfn(h), vfn(h + 1)], axis=1) for vfn in vfns], axis=0)
            o2 = _dot(wh, vh)
            outs += [o2[0:rk, 0:SB_HEAD_DIM], o2[rk:2 * rk, SB_HEAD_DIM:]]
        acc[...] += jnp.concatenate(outs, axis=0)

    @pl.when(p == 0)
    def _():
        acc[...] = jnp.zeros(acc.shape, F32)
        carry[...] = jnp.zeros(carry.shape, F32)
        knp[...] = jnp.zeros(knp.shape, F32)
        vnp[...] = jnp.zeros(vnp.shape, F32)
        knp[0:t, :] = kn_ref[...]
        vnp[0:t, :] = vn_ref[...]
        row_t = lax.broadcasted_iota(jnp.int32, (rows, LANES), 0) % t
        col = lax.broadcasted_iota(jnp.int32, (rows, LANES), 1)
        hd = lambda ref: (lambda h: ref[:, h * SB_HEAD_DIM:(h + 1) * SB_HEAD_DIM].astype(BF16))
        z_new = jnp.concatenate([_dot_nt(qs[hrow(h)], hd(knp)(h)) for h in heads], axis=0)
        blocks([z_new], [hd(vnp)], col < row_t)

    head = lambda ref: (lambda h: ref[pl.ds(h, PAGE_SIZE, stride=SB_KV_HEADS), :].astype(BF16))
    q_t = (q_ref[0] * SB_SCALE).T.astype(BF16)
    col_head = lax.broadcasted_iota(jnp.int32, (SB_HEAD_DIM, rows), 1) // rk
    q_bd = jnp.concatenate([jnp.where(col_head == h, q_t, jnp.zeros((), BF16)) for h in heads], axis=0)
    k_cat = lambda r: jnp.concatenate([head(r)(h) for h in heads], axis=1)
    zs = []
    half = max(pages_per_step // 2, 1)
    for i0 in range(0, pages_per_step, half):
        z_t = _dot(jnp.concatenate([k_cat(r) for r in k_refs[i0:i0 + half]], axis=0), q_bd)
        zs += [z_t[i * PAGE_SIZE:(i + 1) * PAGE_SIZE, :].T for i in range(half)]
    blocks(zs, [head(r) for r in v_refs], None)

    @pl.when(p == pl.num_programs(1) - 1)
    def _():
        o_ref[0] = acc[...].astype(o_ref.dtype)


def sb_sample(q_rows, proj, cache_k, cache_v, layer, page_table, t, pages_per_step=16):
    nseq, n_pages = page_table.shape
    pps = min(pages_per_step, n_pages)
    nsteps = n_pages // pps
    kw = SB_KV_HEADS * SB_HEAD_DIM
    k0 = (A_XBC + SB_HEADS * SB_HEAD_DIM) // kw
    rows = q_rows.shape[1]
    page_rows = PAGE_SIZE * SB_KV_HEADS
    cache_k = cache_k.reshape(cache_k.shape[:2] + (page_rows, SB_HEAD_DIM))
    cache_v = cache_v.reshape(cache_v.shape[:2] + (page_rows, SB_HEAD_DIM))

    def page_spec(i):
        return pl.BlockSpec((None, None, page_rows, SB_HEAD_DIM),
                            lambda b, p, pt: (layer, pt[b, n_pages - 1 - (p * pps + i)], 0, 0))

    grid_spec = pltpu.PrefetchScalarGridSpec(
        num_scalar_prefetch=1,
        grid=(nseq, nsteps),
        in_specs=[pl.BlockSpec((1, rows, SB_HEAD_DIM), lambda b, p, pt: (b, 0, 0)),
                  pl.BlockSpec((t, kw), lambda b, p, pt: (b, k0)),
                  pl.BlockSpec((t, kw), lambda b, p, pt: (b, k0 + 1))]
                 + [page_spec(i) for i in range(pps)] + [page_spec(i) for i in range(pps)],
        out_specs=pl.BlockSpec((1, rows, SB_HEAD_DIM), lambda b, p, pt: (b, 0, 0)),
        scratch_shapes=[pltpu.VMEM((rows, SB_HEAD_DIM), F32), pltpu.VMEM((rows, LANES), F32),
                        pltpu.VMEM((PAGE_SIZE, kw), F32), pltpu.VMEM((PAGE_SIZE, kw), F32)],
    )
    return pl.pallas_call(
        functools.partial(_sb_sample_kernel, pages_per_step=pps, t=t),
        grid_spec=grid_spec,
        out_shape=jax.ShapeDtypeStruct((nseq, rows, SB_HEAD_DIM), BF16),
        compiler_params=_params(("parallel", "arbitrary")),
        name="sb_sample",
    )(page_table, q_rows, proj, proj, *([cache_k] * pps), *([cache_v] * pps))


def _layer_a(xp, xs, batch, seq, nseq, t, cache_k, cache_v, layer, state_ssm, state_conv, page_table,
             g_mix, g_ffn, w_in, w_conv, b_conv, dt_bias, a_log, d_skip, g_ssd, w_out, w1, w2):
    w_main = jnp.concatenate([w_in[:, :A_XBC], w_in[:, A_DT:]], axis=1).astype(BF16)
    w_dt = jnp.pad(w_in[:, A_XBC:A_DT], ((0, 0), (0, LANES - SSD_HEADS))).astype(BF16)
    w_out = w_out.astype(BF16)
    w1 = w1.astype(BF16)
    w2 = w2.astype(BF16)
    gn2 = 2 * SSD_GROUPS * SSD_STATE
    qcol = A_XBC
    kcol = qcol + SB_HEADS * SB_HEAD_DIM
    vcol = kcol + SB_KV_HEADS * SB_HEAD_DIM

    pp = norm_matmul(xp, g_mix, w_main)
    dp = norm_matmul(xp, g_mix, w_dt)
    y_p, ssm_p = ssd_prompt(pp, dp, batch, seq, w_conv, b_conv, dt_bias, a_log, d_skip, g_ssd)
    o_p = sb_prompt(pp, batch, seq)
    xp = out_proj(y_p, o_p, w_out, xp)
    xp = ffn(xp, g_ffn, w1, w2)
    sbk_p = pp[:, kcol:vcol].reshape(batch, seq, SB_KV_HEADS, SB_HEAD_DIM)
    sbv_p = pp[:, vcol:].reshape(batch, seq, SB_KV_HEADS, SB_HEAD_DIM)
    conv_p = pp.reshape(batch, seq, -1)[:, seq - (SSD_CONV - 1):, A_Z:A_Z + SSD_CONV_DIM]
    ssm_p = ssm_p.reshape(batch, SSD_HEADS, SSD_HEAD_DIM, SSD_STATE)

    ps = norm_matmul(xs, g_mix, w_main)
    ds = norm_matmul(xs, g_mix, w_dt)
    y_s, conv_s, ssm_s = ssd_sample(ps, ds, nseq, t, state_conv, state_ssm.reshape(nseq, SSD_D_INNER, SSD_STATE),
                                    w_conv, b_conv, dt_bias, a_log, d_skip, g_ssd)
    q_rows = ps[:, qcol:kcol].reshape(nseq, t, SB_KV_HEADS, SB_Q_PER_KV, SB_HEAD_DIM)
    q_rows = q_rows.transpose(0, 2, 3, 1, 4).reshape(nseq, SB_HEADS * t, SB_HEAD_DIM)
    o_s = sb_sample(q_rows, ps, cache_k, cache_v, layer, page_table, t)
    o_s = o_s.reshape(nseq, SB_KV_HEADS, SB_Q_PER_KV, t, SB_HEAD_DIM).transpose(0, 3, 1, 2, 4)
    o_s = o_s.reshape(nseq * t, SB_HEADS * SB_HEAD_DIM)
    xs = out_proj(y_s, o_s, w_out, xs)
    xs = ffn(xs, g_ffn, w1, w2)
    sbk_s = ps[:, kcol:vcol].reshape(nseq, t, SB_KV_HEADS, SB_HEAD_DIM)
    sbv_s = ps[:, vcol:].reshape(nseq, t, SB_KV_HEADS, SB_HEAD_DIM)
    ssm_s = ssm_s.reshape(nseq, SSD_HEADS, SSD_HEAD_DIM, SSD_STATE)
    return xp, xs, (sbk_p, sbv_p, ssm_p, conv_p), (sbk_s, sbv_s, ssm_s, conv_s)


MLA_PREP_ROWS = 256
C_MAIN = 2 * CMLP_DIM + MLA_Q_RANK + MLA_KV_RANK


def _half_head_ones():
    a = lax.broadcasted_iota(jnp.int32, (LANES, LANES), 0) // MLA_ROPE
    b = lax.broadcasted_iota(jnp.int32, (LANES, LANES), 1) // MLA_ROPE
    return (a == b).astype(BF16)


def _rope_norm(x, g, cos, sin, ones64):
    ms = _dot01_right(x * x, ones64) * (1.0 / MLA_ROPE)
    xn = x * lax.rsqrt(ms + EPS) * g
    lane = lax.broadcasted_iota(jnp.int32, x.shape, 1) % MLA_ROPE
    rot = jnp.where(lane < MLA_ROPE // 2, -pltpu.roll(xn, LANES - MLA_ROPE // 2, 1), pltpu.roll(xn, MLA_ROPE // 2, 1))
    return xn * cos + rot * sin


def _mla_prep_kernel(cq_ref, ckv_ref, kpe_ref, cos_ref, sin_ref, gcq_ref, gckv_ref, gqn_ref, gqr_ref, gkr_ref,
                     wqn_ref, wqr_ref, *rest, with_kv):
    if with_kv:
        gkn_ref, wuk_ref, wuv_ref, qn_ref, qp_ref, ckvn_ref, kper_ref, kn_ref, vv_ref, kp2_ref = rest
    else:
        qn_ref, qp_ref, ckvn_ref, kper_ref = rest
    ones64 = _half_head_ones()
    cos, sin = cos_ref[...], sin_ref[...]
    cqn = _rms(cq_ref[...], gcq_ref[...]).astype(BF16)
    qn = _dot(cqn, wqn_ref[...])
    qp = _dot(cqn, wqr_ref[...])
    for h in range(MLA_HEADS):
        cs = slice(h * MLA_NOPE, (h + 1) * MLA_NOPE)
        qn_ref[:, cs] = (_rms(qn[:, cs], gqn_ref[:, cs]) * MLA_SCALE).astype(qn_ref.dtype)
    for b in range(MLA_HEADS * MLA_ROPE // LANES):
        cs = slice(b * LANES, (b + 1) * LANES)
        qp_ref[:, cs] = (_rope_norm(qp[:, cs], gqr_ref[...], cos, sin, ones64) * MLA_SCALE).astype(qp_ref.dtype)
    ckvn = _rms(ckv_ref[...], gckv_ref[...])
    ckvn_ref[...] = ckvn
    kper = _rope_norm(kpe_ref[...], gkr_ref[...], cos, sin, ones64)
    kper_ref[...] = kper
    if with_kv:
        cb = ckvn.astype(BF16)
        kraw = _dot(cb, wuk_ref[...])
        for h in range(MLA_HEADS):
            cs = slice(h * MLA_NOPE, (h + 1) * MLA_NOPE)
            kn_ref[:, cs] = _rms(kraw[:, cs], gkn_ref[...]).astype(kn_ref.dtype)
        vv_ref[...] = _dot(cb, wuv_ref[...]).astype(vv_ref.dtype)
        kp2_ref[...] = (kper + pltpu.roll(kper, MLA_ROPE, 1)).astype(kp2_ref.dtype)


def mla_prep(proj, kpe_raw, cos, sin, g_cq, g_ckv, g_qn_row, g_qr, g_kr, wq_nope, wq_rope, kv_weights=None):
    m = proj.shape[0]
    tr = min(MLA_PREP_ROWS, m)
    ncos = cos.shape[0] // tr
    c0 = 2 * CMLP_DIM // MLA_Q_RANK
    hq = MLA_HEADS * MLA_NOPE
    hr = MLA_HEADS * MLA_ROPE
    with_kv = kv_weights is not None
    const = lambda shape: pl.BlockSpec(shape, lambda i: (0, 0))
    row = lambda w: pl.BlockSpec((tr, w), lambda i: (i, 0))
    pad_row = lambda v: jnp.concatenate([v, jnp.zeros((LANES - v.shape[0],), F32)]).reshape(1, LANES)
    in_specs = [pl.BlockSpec((tr, MLA_Q_RANK), lambda i: (i, c0)),
                pl.BlockSpec((tr, MLA_KV_RANK), lambda i: (i, c0 + 1)),
                row(LANES),
                pl.BlockSpec((tr, LANES), lambda i: (i % ncos, 0)),
                pl.BlockSpec((tr, LANES), lambda i: (i % ncos, 0)),
                const((1, MLA_Q_RANK)), const((1, MLA_KV_RANK)), const((1, hq)), const((1, LANES)), const((1, LANES)),
                const((MLA_Q_RANK, hq)), const((MLA_Q_RANK, hr))]
    args = [proj, proj, kpe_raw, cos, sin, g_cq.reshape(1, -1), g_ckv.reshape(1, -1), g_qn_row.reshape(1, hq),
            jnp.tile(g_qr, LANES // MLA_ROPE).reshape(1, LANES), pad_row(g_kr), wq_nope, wq_rope]
    out_specs = [row(hq), row(hr), row(MLA_KV_RANK), row(LANES)]
    out_shape = [jax.ShapeDtypeStruct((m, hq), BF16), jax.ShapeDtypeStruct((m, hr), BF16),
                 jax.ShapeDtypeStruct((m, MLA_KV_RANK), F32), jax.ShapeDtypeStruct((m, LANES), F32)]
    if with_kv:
        g_kn, w_uk, w_uv = kv_weights
        in_specs += [const((1, MLA_NOPE)), const((MLA_KV_RANK, hq)), const((MLA_KV_RANK, MLA_HEADS * MLA_V))]
        args += [g_kn.reshape(1, -1), w_uk, w_uv]
        out_specs += [row(hq), row(MLA_HEADS * MLA_V), row(LANES)]
        out_shape += [jax.ShapeDtypeStruct((m, hq), BF16), jax.ShapeDtypeStruct((m, MLA_HEADS * MLA_V), BF16),
                      jax.ShapeDtypeStruct((m, LANES), BF16)]
    return pl.pallas_call(
        functools.partial(_mla_prep_kernel, with_kv=with_kv),
        grid=(m // tr,),
        in_specs=in_specs, out_specs=out_specs, out_shape=out_shape,
        compiler_params=_params(("parallel",)),
        name="mla_prep",
    )(*args)


def _softmax_update(s, v, m_ref, l_ref, acc_ref):
    p, alpha = _softmax_stats(s, m_ref, l_ref)
    _softmax_accumulate(p, alpha, v, acc_ref)


def _softmax_stats(s, m_ref, l_ref):
    m_old = m_ref[...]
    m_new = jnp.maximum(m_old, jnp.max(s, axis=-1, keepdims=True))
    alpha = jnp.exp(m_old - m_new)
    p = jnp.exp(s - m_new[:, :1])
    l_ref[...] = alpha * l_ref[...] + jnp.sum(p, axis=-1, keepdims=True)
    m_ref[...] = m_new
    return p.astype(BF16), alpha[:, :1]


def _softmax_accumulate(p, alpha, v, acc_ref):
    acc_ref[...] = alpha * acc_ref[...] + _dot(p, v)


def _mla_prompt_kernel(qn_ref, qp_ref, kn_ref, kp_ref, vv_ref, o_ref, m_s, l_s, acc_s, *, tq, nh):
    qi = pl.program_id(2)
    lane_head = lax.broadcasted_iota(jnp.int32, (tq, LANES), 1) // MLA_ROPE
    row = lax.broadcasted_iota(jnp.int32, (tq, tq), 0)
    col = lax.broadcasted_iota(jnp.int32, (tq, tq), 1)
    m_s[...] = jnp.full(m_s.shape, -jnp.inf, F32)
    l_s[...] = jnp.zeros(l_s.shape, F32)
    acc_s[...] = jnp.zeros(acc_s.shape, F32)

    def process(j, mask):
        ks = pl.ds(pl.multiple_of(j * tq, tq), tq)
        kp = kp_ref[ks, :]
        head_cols = [slice(hh * MLA_NOPE, (hh + 1) * MLA_NOPE) for hh in range(nh)]
        ones = jnp.ones((tq, LANES), BF16)
        ss = []
        for hh, cs in enumerate(head_cols):
            qp = qp_ref[:, (hh // 2) * LANES:(hh // 2 + 1) * LANES]
            q2 = jnp.where(lane_head == hh % 2, qp, jnp.zeros((), qp.dtype))
            s = _dot_nt(jnp.concatenate([qn_ref[:, cs], q2], axis=1),
                        jnp.concatenate([kn_ref[ks, cs], kp], axis=1))
            ss.append(s if mask is None else jnp.where(mask, s, -jnp.inf))
        pa = []
        for hh, s in enumerate(ss):
            m_old = m_s[hh]
            m_new = jnp.maximum(m_old, jnp.max(s, axis=-1, keepdims=True))
            m_s[hh] = m_new
            pa.append((jnp.exp(s - m_new[:, :1]).astype(BF16), jnp.exp(m_old - m_new)))
        for hh, cs in enumerate(head_cols):
            pv = _dot(pa[hh][0], jnp.concatenate([vv_ref[ks, cs], ones], axis=1))
            acc_s[hh] = pa[hh][1] * acc_s[hh] + pv[:, :MLA_V]
            l_s[hh] = pa[hh][1] * l_s[hh] + pv[:, MLA_V:]

    def body(j, _):
        process(j, None)
        return 0

    lax.fori_loop(0, qi, body, 0)
    process(qi, col <= row)
    for hh in range(nh):
        cs = slice(hh * MLA_NOPE, (hh + 1) * MLA_NOPE)
        o_ref[:, cs] = (acc_s[hh] / l_s[hh][:, :1]).astype(o_ref.dtype)


def mla_prompt(qn, qp, kn, kp2, vv, batch, seq, tq=256, nh=4):
    tq = min(tq, seq)
    nq = seq // tq
    hw = nh * MLA_NOPE
    rw = nh * MLA_ROPE
    return pl.pallas_call(
        functools.partial(_mla_prompt_kernel, tq=tq, nh=nh),
        grid=(batch, MLA_HEADS // nh, nq),
        in_specs=[pl.BlockSpec((tq, hw), lambda b, h, i: (b * nq + i, h)),
                  pl.BlockSpec((tq, rw), lambda b, h, i: (b * nq + i, h)),
                  pl.BlockSpec((seq, hw), lambda b, h, i: (b, h)),
                  pl.BlockSpec((seq, LANES), lambda b, h, i: (b, 0)),
                  pl.BlockSpec((seq, hw), lambda b, h, i: (b, h))],
        out_specs=pl.BlockSpec((tq, hw), lambda b, h, i: (b * nq + i, h)),
        out_shape=jax.ShapeDtypeStruct((batch * seq, MLA_HEADS * MLA_V), BF16),
        scratch_shapes=[pltpu.VMEM((nh, tq, LANES), F32), pltpu.VMEM((nh, tq, LANES), F32),
                        pltpu.VMEM((nh, tq, MLA_V), F32)],
        compiler_params=_params(("parallel", "parallel", "arbitrary")),
        name="mla_prompt",
    )(qn, qp, kn, kp2, vv)


def _head_matmul_kernel(a_ref, w_ref, o_ref):
    a = a_ref[...]
    a = a.reshape(-1, a.shape[-1]).astype(BF16)
    o_ref[...] = _dot(a, w_ref[0]).reshape(o_ref.shape).astype(o_ref.dtype)


def absorb_queries(qg, w_uk_t, nseq, t):
    return pl.pallas_call(
        _head_matmul_kernel,
        grid=(MLA_HEADS,),
        in_specs=[pl.BlockSpec((nseq * t, MLA_NOPE), lambda h: (0, h)),
                  pl.BlockSpec((1, MLA_NOPE, MLA_KV_RANK), lambda h: (h, 0, 0))],
        out_specs=pl.BlockSpec((nseq, None, t, MLA_KV_RANK), lambda h: (0, h, 0, 0)),
        out_shape=jax.ShapeDtypeStruct((nseq, MLA_HEADS, t, MLA_KV_RANK), F32),
        compiler_params=_params(("parallel",)),
        name="absorb_queries",
    )(qg, w_uk_t)


def project_latent_out(o_lat, w_uv_h, nseq, t):
    return pl.pallas_call(
        _head_matmul_kernel,
        grid=(MLA_HEADS,),
        in_specs=[pl.BlockSpec((nseq, None, t, MLA_KV_RANK), lambda h: (0, h, 0, 0)),
                  pl.BlockSpec((1, MLA_KV_RANK, MLA_V), lambda h: (h, 0, 0))],
        out_specs=pl.BlockSpec((nseq * t, MLA_V), lambda h: (0, h)),
        out_shape=jax.ShapeDtypeStruct((nseq * t, MLA_HEADS * MLA_V), BF16),
        compiler_params=_params(("parallel",)),
        name="project_latent_out",
    )(o_lat, w_uv_h)


MLA_BLOCK_PAGES = 8


def _mla_sample_kernel(pt_ref, qa_ref, qp_ref, cn_ref, kn_ref, wuk_ref, *refs, pages_per_step, t):
    c_refs = refs[:pages_per_step]
    k_refs = refs[pages_per_step:2 * pages_per_step]
    o_ref, lhs, m_s, l_s, acc_s, cpad, kpad = refs[2 * pages_per_step:]
    b = pl.program_id(0)
    p = pl.program_id(1)
    nk = MLA_HEADS * MLA_NOPE
    rows = MLA_HEADS * t

    @pl.when(jnp.logical_and(b == 0, p == 0))
    def _():
        lhs[0:nk, :] = wuk_ref[...]

    def front(c):
        keys = c.shape[0]
        kt = _dot_nt(lhs[...], c)
        k2 = kt[0:nk, :]
        k2 = (k2 * k2).reshape(MLA_HEADS, MLA_NOPE // 8, 8, keys).sum(axis=1)
        for sh in (4, 2, 1):
            k2 = k2 + pltpu.roll(k2, sh, 1)
        rs = lax.rsqrt(k2 * (1.0 / MLA_NOPE) + EPS).reshape(rows, keys)
        return kt[nk:nk + rows, :] * rs

    def back(s, c, kpe_t, mask):
        s = s + _dot(qp_ref[0].astype(BF16), kpe_t)
        if mask is not None:
            s = jnp.where(mask, s, -jnp.inf)
        _softmax_update(s, c, m_s, l_s, acc_s)

    @pl.when(p == 0)
    def _():
        lhs[nk:nk + rows, :] = qa_ref[0].reshape(rows, MLA_KV_RANK).astype(BF16)
        m_s[...] = jnp.full(m_s.shape, -jnp.inf, F32)
        l_s[...] = jnp.zeros(l_s.shape, F32)
        acc_s[...] = jnp.zeros(acc_s.shape, F32)
        cpad[...] = jnp.zeros(cpad.shape, F32)
        kpad[...] = jnp.zeros(kpad.shape, F32)
        cpad[0:t, :] = cn_ref[...]
        kpad[0:t, :] = kn_ref[...]
        row_t = lax.broadcasted_iota(jnp.int32, (rows, LANES), 0) % t
        col = lax.broadcasted_iota(jnp.int32, (rows, LANES), 1)
        cn = cpad[...].astype(BF16)
        back(front(cn), cn, kpad[...].T[0:MLA_ROPE, :].astype(BF16), col <= row_t)

    bp = min(MLA_BLOCK_PAGES, pages_per_step)
    latent = lambda i: jnp.concatenate([c_refs[i + k][...] for k in range(bp)], axis=0).astype(BF16)
    rotary = lambda i: jnp.concatenate([k_refs[i + k][...] for k in range(bp)], axis=1).astype(BF16)
    starts = list(range(0, pages_per_step, bp))
    pending = front(latent(starts[0]))
    for n, i in enumerate(starts):
        nxt = front(latent(starts[n + 1])) if n + 1 < len(starts) else None
        back(pending, latent(i), rotary(i), None)
        pending = nxt

    @pl.when(p == pl.num_programs(1) - 1)
    def _():
        o_ref[0] = (acc_s[...] / l_s[:, :1]).reshape(o_ref.shape[1:])


def mla_sample(q_abs, q_pe, ckv_new, kpe_new, w_uk_t2, cache_ckv, cache_kpe, layer, page_table, t, pages_per_step=32):
    nseq, n_pages = page_table.shape
    pps = min(pages_per_step, n_pages)
    nsteps = n_pages // pps
    rows = MLA_HEADS * t
    nk = MLA_HEADS * MLA_NOPE
    cache_kpe_t = jnp.swapaxes(cache_kpe, 2, 3)

    def page_spec(shape, i):
        return pl.BlockSpec((None, None) + shape, lambda b, p, pt: (layer, pt[b, p * pps + i], 0, 0))

    grid_spec = pltpu.PrefetchScalarGridSpec(
        num_scalar_prefetch=1,
        grid=(nseq, nsteps),
        in_specs=[pl.BlockSpec((1, MLA_HEADS, t, MLA_KV_RANK), lambda b, p, pt: (b, 0, 0, 0)),
                  pl.BlockSpec((1, rows, MLA_ROPE), lambda b, p, pt: (b, 0, 0)),
                  pl.BlockSpec((t, MLA_KV_RANK), lambda b, p, pt: (b, 0)),
                  pl.BlockSpec((t, LANES), lambda b, p, pt: (b, 0)),
                  pl.BlockSpec((nk, MLA_KV_RANK), lambda b, p, pt: (0, 0))]
                 + [page_spec((PAGE_SIZE, MLA_KV_RANK), i) for i in range(pps)]
                 + [page_spec((MLA_ROPE, PAGE_SIZE), i) for i in range(pps)],
        out_specs=pl.BlockSpec((1, MLA_HEADS, t, MLA_KV_RANK), lambda b, p, pt: (b, 0, 0, 0)),
        scratch_shapes=[pltpu.VMEM((nk + rows, MLA_KV_RANK), BF16),
                        pltpu.VMEM((rows, LANES), F32), pltpu.VMEM((rows, LANES), F32),
                        pltpu.VMEM((rows, MLA_KV_RANK), F32),
                        pltpu.VMEM((PAGE_SIZE, MLA_KV_RANK), F32), pltpu.VMEM((PAGE_SIZE, LANES), F32)],
    )
    return pl.pallas_call(
        functools.partial(_mla_sample_kernel, pages_per_step=pps, t=t),
        grid_spec=grid_spec,
        out_shape=jax.ShapeDtypeStruct((nseq, MLA_HEADS, t, MLA_KV_RANK), F32),
        compiler_params=_params(("arbitrary", "arbitrary")),
        name="mla_sample",
    )(page_table, q_abs, q_pe, ckv_new, kpe_new, w_uk_t2, *([cache_ckv] * pps), *([cache_kpe_t] * pps))


def _gelu(x):
    return 0.5 * x * (1.0 + jnp.tanh(0.7978845608028654 * (x + 0.044715 * (x * x * x))))


def _cmlp_kernel(u_ref, v_ref, gv_ref, bv_ref, w_ref, bs_ref, o_ref, *rest, seg):
    r = u_ref.shape[0]
    u = _gelu(u_ref[...])
    v = _gelu(v_ref[...])
    mu = jnp.mean(v, axis=-1, keepdims=True)
    vc = v - mu
    var = jnp.mean(vc * vc, axis=-1, keepdims=True)
    vn = vc * lax.rsqrt(var + EPS) * gv_ref[...] + bv_ref[...]
    if rest:
        rest[0][...] = vn
    vb = vn.astype(BF16)
    row = lax.broadcasted_iota(jnp.int32, (r, r), 0)
    col = lax.broadcasted_iota(jnp.int32, (r, r), 1)
    keep = jnp.logical_and(col <= row, (row // seg) == (col // seg))
    gw = CMLP_DIM // CMLP_GROUPS
    for g in range(CMLP_GROUPS):
        cs = slice(g * gw, (g + 1) * gw)
        w = jnp.where(keep, w_ref[g], 0.0).astype(BF16)
        mixed = _dot(w, vb[:, cs]) + bs_ref[:, g:g + 1]
        o_ref[:, cs] = (u[:, cs] * mixed).astype(o_ref.dtype)


def chunk_mlp(proj, g_v, b_v, w_mix, b_mix, seg, emit_v):
    m = proj.shape[0]
    d = CMLP_DIM
    const2 = lambda shape: pl.BlockSpec(shape, lambda i: (0, 0))
    out_specs = [pl.BlockSpec((ROW_TILE, d), lambda i: (i, 0))]
    out_shape = [jax.ShapeDtypeStruct((m, d), BF16)]
    if emit_v:
        out_specs.append(pl.BlockSpec((ROW_TILE, d), lambda i: (i, 0)))
        out_shape.append(jax.ShapeDtypeStruct((m, d), F32))
    return pl.pallas_call(
        functools.partial(_cmlp_kernel, seg=seg),
        grid=(m // ROW_TILE,),
        in_specs=[pl.BlockSpec((ROW_TILE, d), lambda i: (i, 0)),
                  pl.BlockSpec((ROW_TILE, d), lambda i: (i, 1)),
                  const2((1, d)), const2((1, d)),
                  pl.BlockSpec((CMLP_GROUPS, ROW_TILE, ROW_TILE), lambda i: (0, 0, 0)),
                  const2((ROW_TILE, CMLP_GROUPS))],
        out_specs=out_specs, out_shape=out_shape,
        compiler_params=_params(("parallel",)),
        name="chunk_mlp",
    )(proj, proj, g_v.reshape(1, d), b_v.reshape(1, d), w_mix, b_mix)


def _rope_tables(pos):
    half = MLA_ROPE // 2
    inv = ROPE_THETA ** (-jnp.arange(half, dtype=F32) / half)
    ang = pos.astype(F32)[:, None] * inv[None, :]
    reps = LANES // half
    return jnp.tile(jnp.cos(ang), (1, reps)), jnp.tile(jnp.sin(ang), (1, reps))


def _layer_c(xp, xs, batch, seq, nseq, t, cache_ckv, cache_kpe, layer, page_table, g_mix, g_ffn, w_in, g_cq, g_ckv,
             w_uq, g_qn, g_qr, g_kn, g_kr, w_uk, w_uv, g_v, b_v, w_s, b_s, w_out, w1, w2):
    past = page_table.shape[1] * PAGE_SIZE
    w_main = jnp.concatenate([w_in[:, C_KPE:], w_in[:, :C_CKV]], axis=1).astype(BF16)
    w_kpe = jnp.pad(w_in[:, C_CKV:C_KPE], ((0, 0), (0, LANES - MLA_ROPE))).astype(BF16)
    wq = w_uq.reshape(MLA_Q_RANK, MLA_HEADS, MLA_NOPE + MLA_ROPE)
    wq_nope = wq[:, :, :MLA_NOPE].reshape(MLA_Q_RANK, -1).astype(BF16)
    wq_rope = wq[:, :, MLA_NOPE:].reshape(MLA_Q_RANK, -1).astype(BF16)
    w_uk2 = w_uk.reshape(MLA_KV_RANK, -1).astype(BF16)
    w_uv2 = w_uv.reshape(MLA_KV_RANK, -1).astype(BF16)
    w_uk_t = w_uk.transpose(1, 2, 0).astype(BF16)
    w_uv_h = w_uv.transpose(1, 0, 2).astype(BF16)
    w_out = w_out.astype(BF16)
    w1 = w1.astype(BF16)
    w2 = w2.astype(BF16)
    ones_row = jnp.ones((MLA_HEADS * MLA_NOPE,), F32)

    pp = norm_matmul(xp, g_mix, w_main)
    kp = norm_matmul(xp, g_mix, w_kpe)
    cos_p, sin_p = _rope_tables(jnp.arange(seq))
    qn, qp, ckv_p, kpe_p, kn, vv, kp2 = mla_prep(pp, kp, cos_p, sin_p, g_cq, g_ckv, jnp.tile(g_qn, MLA_HEADS) * ones_row,
                                                 g_qr, g_kr, wq_nope, wq_rope, (g_kn, w_uk2, w_uv2))
    o_mla_p = mla_prompt(qn, qp, kn, kp2, vv, batch, seq)
    o_cm_p = chunk_mlp(pp, g_v, b_v, w_s, b_s.T, CMLP_CHUNK, False)[0]
    xp = out_proj(o_mla_p, o_cm_p, w_out, xp)
    xp = ffn(xp, g_ffn, w1, w2)
    ckv_p = ckv_p.reshape(batch, seq, MLA_KV_RANK)
    kpe_p = kpe_p[:, :MLA_ROPE].reshape(batch, seq, MLA_ROPE)

    ps = norm_matmul(xs, g_mix, w_main)
    ks = norm_matmul(xs, g_mix, w_kpe)
    rows_s = nseq * t
    tr = min(MLA_PREP_ROWS, rows_s)
    cos_s, sin_s = _rope_tables(past + (jnp.arange(tr) % t))
    qg, qr, ckv_s, kpe_s = mla_prep(ps, ks, cos_s, sin_s, g_cq, g_ckv, jnp.tile(g_qn * g_kn, MLA_HEADS), g_qr, g_kr,
                                    wq_nope, wq_rope)
    q_abs = absorb_queries(qg, w_uk_t, nseq, t)
    q_pe = qr.reshape(nseq, t, MLA_HEADS, MLA_ROPE).transpose(0, 2, 1, 3).reshape(nseq, MLA_HEADS * t, MLA_ROPE)
    o_lat = mla_sample(q_abs, q_pe.astype(F32), ckv_s, kpe_s, w_uk2.T, cache_ckv, cache_kpe, layer, page_table, t)
    o_mla_s = project_latent_out(o_lat, w_uv_h, nseq, t)
    reps = ROW_TILE // t
    w_mix_s = jnp.tile(w_s[:, :t, :t], (1, reps, reps))
    b_mix_s = jnp.tile(b_s[:, :t].T, (reps, 1))
    o_cm_s, cv_s = chunk_mlp(ps, g_v, b_v, w_mix_s, b_mix_s, t, True)
    xs = out_proj(o_mla_s, o_cm_s, w_out, xs)
    xs = ffn(xs, g_ffn, w1, w2)
    ckv_s = ckv_s.reshape(nseq, t, MLA_KV_RANK)
    kpe_s = kpe_s[:, :MLA_ROPE].reshape(nseq, t, MLA_ROPE)
    return xp, xs, (ckv_p, kpe_p), (ckv_s, kpe_s, cv_s.reshape(nseq, t, CMLP_DIM))


def kernel(x_prompt, x_sample, cache_sb_k, cache_sb_v, cache_mla_ckv, cache_mla_kpe, state_ssm, state_conv, page_table, norm_mix, norm_ffn, w_in_a, w_conv, b_conv, dt_bias, a_log, d_skip, g_ssd, w_out_a, w_in_c, g_cq, g_ckv, w_uq, g_qn, g_qr, g_kn, g_kr, w_uk, w_uv, g_v, b_v, w_s, b_s, w_out_c, w_ffn1, w_ffn2):
    batch, seq, dm = x_prompt.shape
    nseq, t, _ = x_sample.shape
    depth = norm_mix.shape[0]
    xp = x_prompt.reshape(batch * seq, dm)
    xs = x_sample.reshape(nseq * t, dm)
    pa, sa, pc, sc = [], [], [], []
    for l in range(depth):
        i = l // 2
        if l % 2 == 0:
            xp, xs, op, os_ = _layer_a(xp, xs, batch, seq, nseq, t, cache_sb_k, cache_sb_v, i, state_ssm[i], state_conv[i],
                                       page_table, norm_mix[l], norm_ffn[l], w_in_a[i], w_conv[i], b_conv[i], dt_bias[i],
                                       a_log[i], d_skip[i], g_ssd[i], w_out_a[i], w_ffn1[l], w_ffn2[l])
            pa.append(op)
            sa.append(os_)
        else:
            xp, xs, op, os_ = _layer_c(xp, xs, batch, seq, nseq, t, cache_mla_ckv, cache_mla_kpe, i, page_table,
                                       norm_mix[l], norm_ffn[l], w_in_c[i], g_cq[i], g_ckv[i], w_uq[i], g_qn[i], g_qr[i],
                                       g_kn[i], g_kr[i], w_uk[i], w_uv[i], g_v[i], b_v[i], w_s[i], b_s[i], w_out_c[i],
                                       w_ffn1[l], w_ffn2[l])
            pc.append(op)
            sc.append(os_)
    stack = lambda items, k: jnp.stack([it[k] for it in items])
    return (xp.reshape(batch, seq, dm), xs.reshape(nseq, t, dm),
            stack(pa, 0), stack(pa, 1), stack(pa, 2), stack(pa, 3), stack(pc, 0), stack(pc, 1),
            stack(sa, 0), stack(sa, 1), stack(sa, 2), stack(sa, 3), stack(sc, 0), stack(sc, 1), stack(sc, 2))
```

```python
import functools

import jax
import jax.numpy as jnp
from jax import lax
from jax.experimental import pallas as pl
from jax.experimental.pallas import tpu as pltpu

F32 = jnp.float32
BF16 = jnp.bfloat16

D_MODEL = 2048
PAGE_SIZE = 128
SSD_D_INNER = D_MODEL
SSD_HEAD_DIM = 64
SSD_HEADS = SSD_D_INNER // SSD_HEAD_DIM
SSD_GROUPS = 8
SSD_STATE = 128
SSD_CONV = 4
SSD_CHUNK = 128
SSD_CONV_DIM = SSD_D_INNER + 2 * SSD_GROUPS * SSD_STATE
SB_HEAD_DIM = 128
SB_HEADS = D_MODEL // SB_HEAD_DIM
SB_KV_HEADS = 4
SB_Q_PER_KV = SB_HEADS // SB_KV_HEADS
SB_SCALE = SB_HEAD_DIM ** -0.5
MLA_HEADS = 16
MLA_Q_RANK = 512
MLA_KV_RANK = 512
MLA_NOPE = 128
MLA_ROPE = 64
MLA_V = 128
MLA_SCALE = (MLA_NOPE + MLA_ROPE) ** -0.5
ROPE_THETA = 10000.0
CMLP_DIM = D_MODEL
CMLP_GROUPS = 8
CMLP_CHUNK = 128
FFN_DIM = 4 * D_MODEL
EPS = 1e-6

A_Z = SSD_D_INNER
A_XBC = A_Z + SSD_CONV_DIM
A_DT = A_XBC + SSD_HEADS
A_Q = A_DT + SB_HEADS * SB_HEAD_DIM
A_K = A_Q + SB_KV_HEADS * SB_HEAD_DIM
C_CQ = MLA_Q_RANK
C_CKV = C_CQ + MLA_KV_RANK
C_KPE = C_CKV + MLA_ROPE
C_U = C_KPE + CMLP_DIM

LANES = 128
ROW_TILE = 128
VMEM_LIMIT = 56 * 1024 * 1024


def _params(semantics, vmem=VMEM_LIMIT, flags=None):
    return pltpu.CompilerParams(dimension_semantics=semantics, vmem_limit_bytes=vmem, flags=flags)


def _split3(x):
    hi = x.astype(BF16)
    r1 = x - hi.astype(F32)
    mid = r1.astype(BF16)
    lo = (r1 - mid.astype(F32)).astype(BF16)
    return hi, mid, lo


def _dot(a, b):
    return jnp.dot(a, b, preferred_element_type=F32)


def _dot_nt(a, b):
    return lax.dot_general(a, b, (((1,), (1,)), ((), ())), preferred_element_type=F32)


def _dot01_right(x, m01):
    hi, mid, lo = _split3(x)
    return _dot(hi, m01) + _dot(mid, m01) + _dot(lo, m01)


def _dot01_left(m01, x):
    hi, mid, lo = _split3(x)
    return _dot(m01, hi) + _dot(m01, mid) + _dot(m01, lo)


def _softplus(x):
    return jnp.maximum(x, 0.0) + jnp.log1p(jnp.exp(-jnp.abs(x)))


def _silu(x):
    return x * jax.nn.sigmoid(x)


def _rms(x, g):
    ms = jnp.mean(x * x, axis=-1, keepdims=True)
    return x * lax.rsqrt(ms + EPS) * g


def _norm_matmul_kernel(x_ref, g_ref, w_ref, o_ref, h_ref):
    @pl.when(pl.program_id(1) == 0)
    def _():
        h_ref[...] = _rms(x_ref[...], g_ref[...]).astype(BF16)

    o_ref[...] = _dot(h_ref[...], w_ref[...])


def norm_matmul(x, g, w, tm=1024, tn=512):
    m, k = x.shape
    n = w.shape[1]
    tm, tn = min(tm, m), min(tn, n)
    return pl.pallas_call(
        _norm_matmul_kernel,
        grid=(m // tm, n // tn),
        in_specs=[pl.BlockSpec((tm, k), lambda i, j: (i, 0)),
                  pl.BlockSpec((1, k), lambda i, j: (0, 0)),
                  pl.BlockSpec((k, tn), lambda i, j: (0, j))],
        out_specs=pl.BlockSpec((tm, tn), lambda i, j: (i, j)),
        out_shape=jax.ShapeDtypeStruct((m, n), F32),
        scratch_shapes=[pltpu.VMEM((tm, k), BF16)],
        compiler_params=_params(("parallel", "arbitrary")),
        name="norm_matmul",
    )(x, g.reshape(1, k), w)


def _out_proj_kernel(a_ref, b_ref, wa_ref, wb_ref, r_ref, o_ref):
    o_ref[...] = r_ref[...] + (_dot(a_ref[...], wa_ref[...]) + _dot(b_ref[...], wb_ref[...]))


def out_proj(a, b, w, res, tm=1024, tn=512):
    m, kh = a.shape
    n = w.shape[1]
    tm, tn = min(tm, m), min(tn, n)
    return pl.pallas_call(
        _out_proj_kernel,
        grid=(m // tm, n // tn),
        in_specs=[pl.BlockSpec((tm, kh), lambda i, j: (i, 0)),
                  pl.BlockSpec((tm, kh), lambda i, j: (i, 0)),
                  pl.BlockSpec((kh, tn), lambda i, j: (0, j)),
                  pl.BlockSpec((kh, tn), lambda i, j: (1, j)),
                  pl.BlockSpec((tm, tn), lambda i, j: (i, j))],
        out_specs=pl.BlockSpec((tm, tn), lambda i, j: (i, j)),
        out_shape=jax.ShapeDtypeStruct((m, n), F32),
        compiler_params=_params(("parallel", "arbitrary")),
        name="out_proj",
    )(a, b, w, w, res)


def _ffn_kernel(x_ref, g_ref, w1_ref, w2_ref, o_ref, h_ref):
    @pl.when(pl.program_id(1) == 0)
    def _():
        x = x_ref[...]
        h_ref[...] = _rms(x, g_ref[...]).astype(BF16)
        o_ref[...] = x

    a = jnp.maximum(_dot(h_ref[...], w1_ref[...]), 0.0)
    o_ref[...] += _dot((a * a).astype(BF16), w2_ref[...])


def ffn(x, g, w1, w2, layer, tm=1024, tf=512):
    m, d = x.shape
    f = w1.shape[2]
    tm, tf = min(tm, m), min(tf, f)
    return pl.pallas_call(
        _ffn_kernel,
        grid=(m // tm, f // tf),
        in_specs=[pl.BlockSpec((tm, d), lambda i, j: (i, 0)),
                  pl.BlockSpec((1, d), lambda i, j: (0, 0)),
                  pl.BlockSpec((None, d, tf), lambda i, j: (layer, 0, j)),
                  pl.BlockSpec((None, tf, d), lambda i, j: (layer, j, 0))],
        out_specs=pl.BlockSpec((tm, d), lambda i, j: (i, 0)),
        out_shape=jax.ShapeDtypeStruct((m, d), F32),
        scratch_shapes=[pltpu.VMEM((tm, d), BF16)],
        compiler_params=_params(("parallel", "arbitrary")),
        name="ffn",
    )(x, g.reshape(1, d), w1, w2)


def _head_expand():
    h = lax.broadcasted_iota(jnp.int32, (LANES, SSD_D_INNER), 0)
    c = lax.broadcasted_iota(jnp.int32, (LANES, SSD_D_INNER), 1)
    return (c // SSD_HEAD_DIM == h).astype(BF16)


def _conv_silu(pad_ref, w_ref, b_ref, lead, rows):
    acc = b_ref[...] + w_ref[SSD_CONV - 1:SSD_CONV, :] * pad_ref[lead:lead + rows, :]
    for k in range(SSD_CONV - 1):
        s = lead - (SSD_CONV - 1) + k
        acc = acc + w_ref[k:k + 1, :] * pad_ref[s:s + rows, :]
    return _silu(acc)


def _ssd_intra(xs, bc, z_unused, dt_raw, dtb, alog, e01, same_block):
    r = xs.shape[0]
    row = lax.broadcasted_iota(jnp.int32, (r, r), 0)
    col = lax.broadcasted_iota(jnp.int32, (r, r), 1)
    causal = jnp.logical_and(col <= row, same_block)
    dt = _softplus(dt_raw + dtb)
    dta = dt * (-jnp.exp(alog))
    at = _dot01_left(causal.astype(BF16), dta)
    at_t = at.T
    at_x = _dot01_right(at, e01)
    dt_x = _dot01_right(dt, e01)
    xdt = xs * dt_x
    return dict(causal=causal, at=at, at_t=at_t, at_x=at_x, xdt=xdt)


def _ssd_diag_pair(q, pr, cb, at, at_t, causal, xdt_bf):
    lane = lax.broadcasted_iota(jnp.int32, (q, LANES), 1)
    ys = []
    for h in (2 * pr, 2 * pr + 1):
        d = at[:, h:h + 1] - at_t[h:h + 1, :]
        dec = jnp.exp(jnp.where(causal, d, -jnp.inf))
        ys.append(_dot((cb * dec).astype(BF16), xdt_bf))
    return jnp.where(lane < SSD_HEAD_DIM, ys[0], ys[1])


def _gated_group_norm(y, z, g_ref, o_ref):
    gw = SSD_D_INNER // SSD_GROUPS
    for g in range(SSD_GROUPS):
        cs = slice(g * gw, (g + 1) * gw)
        yg = y[:, cs] * _silu(z[:, cs])
        o_ref[:, cs] = _rms(yg, g_ref[:, cs]).astype(o_ref.dtype)


def _ssd_prompt_kernel(z_ref, xs_ref, bc_ref, dt_ref, wcx_ref, wcb_ref, bcx_ref, bcb_ref, dtb_ref,
                       alog_ref, dsk_ref, gss_ref, e_ref, y_ref, s_ref, padx, padb, st, ybuf):
    c = pl.program_id(1)
    q = ROW_TILE
    lead = 8
    gn = SSD_GROUPS * SSD_STATE

    @pl.when(c == 0)
    def _():
        padx[0:lead, :] = jnp.zeros((lead, SSD_D_INNER), F32)
        padb[0:lead, :] = jnp.zeros((lead, 2 * gn), F32)
        st[...] = jnp.zeros(st.shape, F32)

    padx[lead:lead + q, :] = xs_ref[...]
    padb[lead:lead + q, :] = bc_ref[...]
    xs = _conv_silu(padx, wcx_ref, bcx_ref, lead, q)
    bc = _conv_silu(padb, wcb_ref, bcb_ref, lead, q)
    padx[0:lead, :] = padx[q:q + lead, :]
    padb[0:lead, :] = padb[q:q + lead, :]

    e01 = e_ref[...]
    whole = jnp.full((q, q), True)
    r = _ssd_intra(xs, bc, None, dt_ref[...], dtb_ref[...], alog_ref[...], e01, whole)
    at, at_t, at_x, xdt, causal = r["at"], r["at_t"], r["at_x"], r["xdt"], r["causal"]
    eat = jnp.exp(at_x)
    last = at_x[q - 1:q, :]
    xw = (xdt * jnp.exp(last - at_x)).astype(BF16)
    cd = jnp.exp(last)
    xdt_bf = xdt.astype(BF16)

    for g in range(SSD_GROUPS):
        bm = bc[:, g * SSD_STATE:(g + 1) * SSD_STATE]
        cm = bc[:, gn + g * SSD_STATE:gn + (g + 1) * SSD_STATE].astype(BF16)
        cb = _dot_nt(cm, bm.astype(BF16))
        bm_t = bm.T.astype(BF16)
        for pr in (2 * g, 2 * g + 1):
            cs = slice(pr * LANES, (pr + 1) * LANES)
            yd = _ssd_diag_pair(q, pr, cb, at, at_t, causal, xdt_bf[:, cs])
            s_in = st[:, cs]
            yo = _dot(cm, s_in.astype(BF16)) * eat[:, cs]
            st[:, cs] = s_in * cd[:, cs] + _dot(bm_t, xw[:, cs])
            ybuf[:, cs] = yd + yo

    y = ybuf[...] + dsk_ref[...] * xs
    _gated_group_norm(y, z_ref[...], gss_ref, y_ref)

    @pl.when(c == pl.num_programs(1) - 1)
    def _():
        for pr in range(SSD_D_INNER // LANES):
            cs = slice(pr * LANES, (pr + 1) * LANES)
            s_ref[0, cs, :] = st[:, cs].T


def ssd_prompt(proj, dtp, batch, seq, w_conv, b_conv, dt_bias, a_log, d_skip, g_ssd):
    nc = seq // ROW_TILE
    gn = SSD_GROUPS * SSD_STATE
    d = SSD_D_INNER
    row = lambda b, c: b * nc + c
    const = lambda shape: pl.BlockSpec(shape, lambda b, c: (0, 0))
    pad128 = lambda v: jnp.pad(v, (0, LANES - v.shape[0])).reshape(1, LANES)
    return pl.pallas_call(
        _ssd_prompt_kernel,
        grid=(batch, nc),
        in_specs=[pl.BlockSpec((ROW_TILE, d), lambda b, c: (row(b, c), 0)),
                  pl.BlockSpec((ROW_TILE, d), lambda b, c: (row(b, c), 1)),
                  pl.BlockSpec((ROW_TILE, 2 * gn), lambda b, c: (row(b, c), 2)),
                  pl.BlockSpec((ROW_TILE, LANES), lambda b, c: (row(b, c), 0)),
                  const((SSD_CONV, d)), const((SSD_CONV, 2 * gn)), const((1, d)), const((1, 2 * gn)),
                  const((1, LANES)), const((1, LANES)), const((1, d)), const((1, d)),
                  const((LANES, d))],
        out_specs=[pl.BlockSpec((ROW_TILE, d), lambda b, c: (row(b, c), 0)),
                   pl.BlockSpec((1, d, SSD_STATE), lambda b, c: (b, 0, 0))],
        out_shape=[jax.ShapeDtypeStruct((batch * seq, d), BF16),
                   jax.ShapeDtypeStruct((batch, d, SSD_STATE), F32)],
        scratch_shapes=[pltpu.VMEM((ROW_TILE + 8, d), F32), pltpu.VMEM((ROW_TILE + 8, 2 * gn), F32),
                        pltpu.VMEM((SSD_STATE, d), F32), pltpu.VMEM((ROW_TILE, d), F32)],
        compiler_params=_params(("parallel", "arbitrary")),
        name="ssd_prompt",
    )(proj, proj, proj, dtp, w_conv[:, :d], w_conv[:, d:], b_conv[:d].reshape(1, d),
      b_conv[d:].reshape(1, 2 * gn), pad128(dt_bias), pad128(a_log),
      jnp.repeat(d_skip, SSD_HEAD_DIM).reshape(1, d), g_ssd.reshape(1, d), _head_expand())


def _ssd_sample_kernel(z_ref, xs_ref, bc_ref, dt_ref, cvx_ref, cvb_ref, s0_ref, wcx_ref, wcb_ref,
                       bcx_ref, bcb_ref, dtb_ref, alog_ref, dsk_ref, gss_ref, e_ref, et_ref,
                       y_ref, cox_ref, cob_ref, s_ref,
                       padx, padb, ybuf, yoff, eat_s, cm_s, bm_s, xwt_s, att_s):
    j = pl.program_id(1)
    r = ROW_TILE
    t = xs_ref.shape[0] // cvx_ref.shape[0]
    nseq = r // t
    lead = 8
    gn = SSD_GROUPS * SSD_STATE
    d = SSD_D_INNER
    hist = SSD_CONV - 1

    @pl.when(j == 0)
    def _():
        def conv(pad, raw_ref, cv_ref, w_ref, b_ref, co_ref):
            width = raw_ref.shape[1]
            pad[:, lead - hist:lead, :] = cv_ref[...]
            pad[:, lead:lead + t, :] = raw_ref[...].reshape(nseq, t, width)
            acc = b_ref[...] + w_ref[hist:hist + 1, :] * pad[:, lead:lead + t, :]
            for k in range(hist):
                s = lead - hist + k
                acc = acc + w_ref[k:k + 1, :] * pad[:, s:s + t, :]
            co_ref[...] = pad[:, lead + t - hist:lead + t, :]
            return _silu(acc).reshape(r, width)

        xs = conv(padx, xs_ref, cvx_ref, wcx_ref, bcx_ref, cox_ref)
        bc = conv(padb, bc_ref, cvb_ref, wcb_ref, bcb_ref, cob_ref)
        e01 = e_ref[...]
        row = lax.broadcasted_iota(jnp.int32, (r, r), 0)
        col = lax.broadcasted_iota(jnp.int32, (r, r), 1)
        same = (row // t) == (col // t)
        q = _ssd_intra(xs, bc, None, dt_ref[...], dtb_ref[...], alog_ref[...], e01, same)
        at, at_t, at_x, xdt, causal = q["at"], q["at_t"], q["at_x"], q["xdt"], q["causal"]
        eat_s[...] = jnp.exp(at_x)
        last_sel = jnp.logical_and(same, (col % t) == t - 1).astype(BF16)
        last = _dot01_left(last_sel, at_x)
        xw = xdt * jnp.exp(last - at_x)
        xdt_bf = xdt.astype(BF16)
        att_s[...] = _dot01_left(et_ref[...], at_t)
        for g in range(SSD_GROUPS):
            bm = bc[:, g * SSD_STATE:(g + 1) * SSD_STATE].astype(BF16)
            cm = bc[:, gn + g * SSD_STATE:gn + (g + 1) * SSD_STATE].astype(BF16)
            bm_s[:, g * SSD_STATE:(g + 1) * SSD_STATE] = bm
            cm_s[:, g * SSD_STATE:(g + 1) * SSD_STATE] = cm
            cb = _dot_nt(cm, bm)
            for pr in (2 * g, 2 * g + 1):
                cs = slice(pr * LANES, (pr + 1) * LANES)
                ybuf[:, cs] = _ssd_diag_pair(r, pr, cb, at, at_t, causal, xdt_bf[:, cs])
                xwt_s[cs, :] = xw[:, cs].T.astype(BF16)
        ybuf[...] = ybuf[...] + dsk_ref[...] * xs
        yoff[...] = jnp.zeros(yoff.shape, F32)

    rowseq = lax.broadcasted_iota(jnp.int32, (r, LANES), 0) // t
    laneseq = lax.broadcasted_iota(jnp.int32, (LANES, r), 1) // t
    lastlane = lax.broadcasted_iota(jnp.int32, (LANES, r), 1) == j * t + t - 1
    gw = d // SSD_GROUPS
    for g in range(SSD_GROUPS):
        rs = slice(g * gw, (g + 1) * gw)
        gs = slice(g * SSD_STATE, (g + 1) * SSD_STATE)
        s0 = s0_ref[0, rs, :]
        yo = _dot_nt(cm_s[:, gs], s0.astype(BF16))
        for half in range(gw // LANES):
            cs = slice(g * gw + half * LANES, g * gw + (half + 1) * LANES)
            yoff[:, cs] += jnp.where(rowseq == j, yo[:, half * LANES:(half + 1) * LANES], 0.0)
        sel = jnp.concatenate([laneseq, laneseq], axis=0) == j
        sc = _dot(jnp.where(sel, xwt_s[rs, :], jnp.zeros((), BF16)), bm_s[:, gs])
        lastc = jnp.concatenate([lastlane, lastlane], axis=0)
        cdc = jnp.exp(jnp.sum(jnp.where(lastc, att_s[rs, :], 0.0), axis=1, keepdims=True))
        s_ref[0, rs, :] = s0 * cdc + sc

    @pl.when(j == pl.num_programs(1) - 1)
    def _():
        y = ybuf[...] + yoff[...] * eat_s[...]
        _gated_group_norm(y, z_ref[...], gss_ref, y_ref)


def ssd_sample(proj, dtp, nseq, t, state_conv, state_ssm, w_conv, b_conv, dt_bias, a_log, d_skip, g_ssd):
    spb = ROW_TILE // t
    nb = nseq // spb
    gn = SSD_GROUPS * SSD_STATE
    d = SSD_D_INNER
    hist = SSD_CONV - 1
    const = lambda shape: pl.BlockSpec(shape, lambda i, j: (0, 0))
    pad128 = lambda v: jnp.pad(v, (0, LANES - v.shape[0])).reshape(1, LANES)
    e01 = _head_expand()
    outs = pl.pallas_call(
        _ssd_sample_kernel,
        grid=(nb, spb),
        in_specs=[pl.BlockSpec((ROW_TILE, d), lambda i, j: (i, 0)),
                  pl.BlockSpec((ROW_TILE, d), lambda i, j: (i, 1)),
                  pl.BlockSpec((ROW_TILE, 2 * gn), lambda i, j: (i, 2)),
                  pl.BlockSpec((ROW_TILE, LANES), lambda i, j: (i, 0)),
                  pl.BlockSpec((spb, hist, d), lambda i, j: (i, 0, 0)),
                  pl.BlockSpec((spb, hist, 2 * gn), lambda i, j: (i, 0, 1)),
                  pl.BlockSpec((1, d, SSD_STATE), lambda i, j: (i * spb + j, 0, 0)),
                  const((SSD_CONV, d)), const((SSD_CONV, 2 * gn)), const((1, d)), const((1, 2 * gn)),
                  const((1, LANES)), const((1, LANES)), const((1, d)), const((1, d)),
                  const((LANES, d)), const((d, LANES))],
        out_specs=[pl.BlockSpec((ROW_TILE, d), lambda i, j: (i, 0)),
                   pl.BlockSpec((spb, hist, d), lambda i, j: (i, 0, 0)),
                   pl.BlockSpec((spb, hist, 2 * gn), lambda i, j: (i, 0, 0)),
                   pl.BlockSpec((1, d, SSD_STATE), lambda i, j: (i * spb + j, 0, 0))],
        out_shape=[jax.ShapeDtypeStruct((nseq * t, d), BF16),
                   jax.ShapeDtypeStruct((nseq, hist, d), F32),
                   jax.ShapeDtypeStruct((nseq, hist, 2 * gn), F32),
                   jax.ShapeDtypeStruct((nseq, d, SSD_STATE), F32)],
        scratch_shapes=[pltpu.VMEM((spb, 8 + t, d), F32), pltpu.VMEM((spb, 8 + t, 2 * gn), F32),
                        pltpu.VMEM((ROW_TILE, d), F32), pltpu.VMEM((ROW_TILE, d), F32),
                        pltpu.VMEM((ROW_TILE, d), F32),
                        pltpu.VMEM((ROW_TILE, gn), BF16), pltpu.VMEM((ROW_TILE, gn), BF16),
                        pltpu.VMEM((d, ROW_TILE), BF16), pltpu.VMEM((d, ROW_TILE), F32)],
        compiler_params=_params(("parallel", "arbitrary")),
        name="ssd_sample",
    )(proj, proj, proj, dtp, state_conv, state_conv, state_ssm,
      w_conv[:, :d], w_conv[:, d:], b_conv[:d].reshape(1, d), b_conv[d:].reshape(1, 2 * gn),
      pad128(dt_bias), pad128(a_log), jnp.repeat(d_skip, SSD_HEAD_DIM).reshape(1, d),
      g_ssd.reshape(1, d), e01, e01.T)
    y, cox, cob, s_new = outs
    return y, jnp.concatenate([cox, cob], axis=-1), s_new


SB_DEAD = -120.0


def _later_matrix():
    j = lax.broadcasted_iota(jnp.int32, (2 * LANES, 2 * LANES), 0) % LANES
    s = lax.broadcasted_iota(jnp.int32, (2 * LANES, 2 * LANES), 1)
    return jnp.where(jnp.logical_or(j > s, s >= LANES), -1.0, 0.0).astype(BF16)


def _sb_weights(zs, vis, carry, later01):
    assert vis is None or len(zs) == 1
    rows = zs[0].shape[0]
    z = jnp.concatenate(zs, axis=0)
    sp = jnp.maximum(z, 0.0) + jnp.log(1.0 + jnp.exp(-jnp.abs(z)))
    ls = z - sp
    if vis is not None:
        sp = jnp.where(vis, sp, 0.0)
    hi = sp.astype(BF16)
    lo = (sp - hi.astype(F32)).astype(BF16)
    lc = _dot(jnp.concatenate([hi, lo], axis=1), later01)
    ws = []
    for i in range(len(zs)):
        rs = slice(i * rows, (i + 1) * rows)
        w = jnp.exp(ls[rs] + lc[rs, :LANES] + carry)
        ws.append(w if vis is None else jnp.where(vis, w, 0.0))
        carry = carry + lc[rs, LANES:]
    return ws, carry


def _sb_prompt_kernel(q_ref, k_ref, v_ref, o_ref, acc, carry):
    qi = pl.program_id(2)
    t = ROW_TILE
    g4 = SB_Q_PER_KV
    later01 = _later_matrix()
    q = q_ref[...]
    qs = jnp.concatenate([q[:, g * SB_HEAD_DIM:(g + 1) * SB_HEAD_DIM] for g in range(g4)], axis=0)
    qs = (qs * SB_SCALE).astype(BF16)
    rows = g4 * t
    chunks = 2
    hr = rows // chunks

    def kv_rows(j):
        return pl.ds(j * t, t) if isinstance(j, int) else pl.ds(pl.multiple_of(j * t, t), t)

    def step(js, vis):
        ks = [k_ref[kv_rows(j), :].astype(BF16) for j in js]
        vs = jnp.concatenate([v_ref[kv_rows(j), :].astype(BF16) for j in js], axis=0)
        zs = [[_dot_nt(qs[c * hr:(c + 1) * hr], k) for k in ks] for c in range(chunks)]
        wcat = []
        for c in range(chunks):
            rs = slice(c * hr, (c + 1) * hr)
            ws, cn = _sb_weights(zs[c], vis, carry[rs, :], later01)
            carry[rs, :] = cn
            wcat.append(jnp.concatenate([w.astype(BF16) for w in ws], axis=1))
        for c in range(chunks):
            acc[c * hr:(c + 1) * hr, :] += _dot(wcat[c], vs)

    acc[...] = jnp.zeros(acc.shape, F32)
    carry[...] = jnp.zeros(carry.shape, F32)
    row_t = lax.broadcasted_iota(jnp.int32, (hr, t), 0) % t
    col = lax.broadcasted_iota(jnp.int32, (hr, t), 1)
    step([qi], col < row_t)

    alive = lambda: jnp.max(carry[...]) > SB_DEAD

    def cond(state):
        i, live = state
        return jnp.logical_and(i < qi // 2, live)

    def body(state):
        i, _ = state
        j = qi - 1 - 2 * i
        step([j, j - 1], None)
        return i + 1, alive()

    _, live = lax.while_loop(cond, body, (jnp.int32(0), alive()))

    @pl.when(jnp.logical_and(qi % 2 == 1, live))
    def _():
        step([0], None)

    for g in range(g4):
        o_ref[:, g * SB_HEAD_DIM:(g + 1) * SB_HEAD_DIM] = acc[g * t:(g + 1) * t, :].astype(o_ref.dtype)


def sb_prompt(proj, batch, seq):
    nq = seq // ROW_TILE
    qw = SB_Q_PER_KV * SB_HEAD_DIM
    q0 = (A_XBC) // qw
    k0 = (A_XBC + SB_HEADS * SB_HEAD_DIM) // SB_HEAD_DIM
    v0 = k0 + SB_KV_HEADS
    return pl.pallas_call(
        _sb_prompt_kernel,
        grid=(batch, SB_KV_HEADS, nq),
        in_specs=[pl.BlockSpec((ROW_TILE, qw), lambda b, h, i: (b * nq + i, q0 + h)),
                  pl.BlockSpec((seq, SB_HEAD_DIM), lambda b, h, i: (b, k0 + h)),
                  pl.BlockSpec((seq, SB_HEAD_DIM), lambda b, h, i: (b, v0 + h))],
        out_specs=pl.BlockSpec((ROW_TILE, qw), lambda b, h, i: (b * nq + i, h)),
        out_shape=jax.ShapeDtypeStruct((batch * seq, SB_HEADS * SB_HEAD_DIM), BF16),
        scratch_shapes=[pltpu.VMEM((SB_Q_PER_KV * ROW_TILE, SB_HEAD_DIM), F32),
                        pltpu.VMEM((SB_Q_PER_KV * ROW_TILE, LANES), F32)],
        compiler_params=_params(("parallel", "parallel", "arbitrary")),
        name="sb_prompt",
    )(proj, proj, proj)


def _sb_sample_kernel(*refs, pages_per_step, t, first):
    if first:
        pt_ref, q_ref, kn_ref, vn_ref = refs[:4]
        refs = refs[4:]
    else:
        pt_ref, dead_ref, q_ref, acc_in, carry_in = refs[:5]
        refs = refs[5:]
    k_refs = refs[:pages_per_step]
    v_refs = refs[pages_per_step:2 * pages_per_step]
    if first:
        acc_out, carry_out, dead_out, acc, carry, knp, vnp = refs[2 * pages_per_step:]
    else:
        o_ref, acc, carry = refs[2 * pages_per_step:]
    rows = q_ref.shape[1]
    rk = rows // SB_KV_HEADS
    later01 = _later_matrix()
    qs = (q_ref[0] * SB_SCALE).astype(BF16)

    heads = range(SB_KV_HEADS)
    hrow = lambda h: slice(h * rk, (h + 1) * rk)

    def blocks(zs, vfns, vis):
        ws, c = _sb_weights(zs, vis, carry[...], later01)
        carry[...] = c
        ws = [w.astype(BF16) for w in ws]
        outs = []
        for h in range(0, SB_KV_HEADS, 2):
            wh = jnp.concatenate([w[h * rk:(h + 2) * rk] for w in ws], axis=1)
            vh = jnp.concatenate([jnp.concatenate([vfn(h), vfn(h + 1)], axis=1) for vfn in vfns], axis=0)
            o2 = _dot(wh, vh)
            outs += [o2[0:rk, 0:SB_HEAD_DIM], o2[rk:2 * rk, SB_HEAD_DIM:]]
        acc[...] += jnp.concatenate(outs, axis=0)

    def new_tokens():
        knp[...] = jnp.zeros(knp.shape, F32)
        vnp[...] = jnp.zeros(vnp.shape, F32)
        knp[0:t, :] = kn_ref[...]
        vnp[0:t, :] = vn_ref[...]
        row_t = lax.broadcasted_iota(jnp.int32, (rows, LANES), 0) % t
        col = lax.broadcasted_iota(jnp.int32, (rows, LANES), 1)
        hd = lambda ref: (lambda h: ref[:, h * SB_HEAD_DIM:(h + 1) * SB_HEAD_DIM].astype(BF16))
        z_new = jnp.concatenate([_dot_nt(qs[hrow(h)], hd(knp)(h)) for h in heads], axis=0)
        blocks([z_new], [hd(vnp)], col < row_t)

    def cached_pages():
        head = lambda ref: (lambda h: ref[pl.ds(h, PAGE_SIZE, stride=SB_KV_HEADS), :].astype(BF16))
        q_t = (q_ref[0] * SB_SCALE).T.astype(BF16)
        col_head = lax.broadcasted_iota(jnp.int32, (SB_HEAD_DIM, rows), 1) // rk
        q_bd = jnp.concatenate([jnp.where(col_head == h, q_t, jnp.zeros((), BF16)) for h in heads], axis=0)
        k_cat = lambda r: jnp.concatenate([head(r)(h) for h in heads], axis=1)
        zs = []
        half = (pages_per_step + 1) // 2
        for i0 in range(0, pages_per_step, half):
            part = k_refs[i0:i0 + half]
            z_t = _dot(jnp.concatenate([k_cat(r) for r in part], axis=0), q_bd)
            zs += [z_t[i * PAGE_SIZE:(i + 1) * PAGE_SIZE, :].T for i in range(len(part))]
        blocks(zs, [head(r) for r in v_refs], None)

    if first:
        acc[...] = jnp.zeros(acc.shape, F32)
        carry[...] = jnp.zeros(carry.shape, F32)
        new_tokens()
        cached_pages()
        acc_out[0] = acc[...]
        carry_out[0] = carry[...]
        dead_out[...] = jnp.full(dead_out.shape, jnp.where(jnp.max(carry[...]) > SB_DEAD, 0, 1), jnp.int32)
    else:
        b = pl.program_id(0)
        p = pl.program_id(1)

        @pl.when(p == 0)
        def _():
            acc[...] = acc_in[0]
            carry[...] = carry_in[0]

        @pl.when(jnp.logical_and(dead_ref[b] == 0, jnp.max(carry[...]) > SB_DEAD))
        def _():
            cached_pages()

        @pl.when(p == pl.num_programs(1) - 1)
        def _():
            o_ref[0] = acc[...].astype(o_ref.dtype)


def sb_sample(q_rows, proj, cache_k, cache_v, layer, page_table, t, first_pages=4, pages_per_step=20):
    nseq, n_pages = page_table.shape
    kw = SB_KV_HEADS * SB_HEAD_DIM
    k0 = (A_XBC + SB_HEADS * SB_HEAD_DIM) // kw
    rows = q_rows.shape[1]
    page_rows = PAGE_SIZE * SB_KV_HEADS
    cache_k = cache_k.reshape(cache_k.shape[:2] + (page_rows, SB_HEAD_DIM))
    cache_v = cache_v.reshape(cache_v.shape[:2] + (page_rows, SB_HEAD_DIM))
    g = min(first_pages, n_pages)
    row_spec = lambda: pl.BlockSpec((1, rows, SB_HEAD_DIM), lambda b, *_: (b, 0, 0))
    page_block = (None, None, page_rows, SB_HEAD_DIM)

    first_page = lambda i: pl.BlockSpec(page_block, lambda b, pt: (layer, pt[b, n_pages - 1 - i], 0, 0))
    acc, carry, dead = pl.pallas_call(
        functools.partial(_sb_sample_kernel, pages_per_step=g, t=t, first=True),
        grid_spec=pltpu.PrefetchScalarGridSpec(
            num_scalar_prefetch=1,
            grid=(nseq,),
            in_specs=[row_spec(),
                      pl.BlockSpec((t, kw), lambda b, pt: (b, k0)),
                      pl.BlockSpec((t, kw), lambda b, pt: (b, k0 + 1))]
                     + [first_page(i) for i in range(g)] + [first_page(i) for i in range(g)],
            out_specs=[row_spec(), row_spec(), pl.BlockSpec((1, 8, LANES), lambda b, pt: (b, 0, 0))],
            scratch_shapes=[pltpu.VMEM((rows, SB_HEAD_DIM), F32), pltpu.VMEM((rows, LANES), F32),
                            pltpu.VMEM((PAGE_SIZE, kw), F32), pltpu.VMEM((PAGE_SIZE, kw), F32)]),
        out_shape=[jax.ShapeDtypeStruct((nseq, rows, SB_HEAD_DIM), F32),
                   jax.ShapeDtypeStruct((nseq, rows, LANES), F32),
                   jax.ShapeDtypeStruct((nseq, 8, LANES), jnp.int32)],
        compiler_params=_params(("arbitrary",)),
        name="sb_sample_recent",
    )(page_table, q_rows, proj, proj, *([cache_k] * g), *([cache_v] * g))
    rem = n_pages - g
    if rem == 0:
        return acc.astype(BF16)
    pps = max(d for d in range(1, min(pages_per_step, rem) + 1) if rem % d == 0)

    def old_page(i):
        return pl.BlockSpec(page_block, lambda b, p, pt, dn: (
            layer, jnp.where(dn[b] > 0, 0, pt[b, rem - 1 - (p * pps + i)]), 0, 0))

    older = pl.pallas_call(
        functools.partial(_sb_sample_kernel, pages_per_step=pps, t=t, first=False),
        grid_spec=pltpu.PrefetchScalarGridSpec(
            num_scalar_prefetch=2,
            grid=(nseq, rem // pps),
            in_specs=[row_spec(), row_spec(), row_spec()]
                     + [old_page(i) for i in range(pps)] + [old_page(i) for i in range(pps)],
            out_specs=row_spec(),
            scratch_shapes=[pltpu.VMEM((rows, SB_HEAD_DIM), F32), pltpu.VMEM((rows, LANES), F32)]),
        out_shape=jax.ShapeDtypeStruct((nseq, rows, SB_HEAD_DIM), BF16),
        compiler_params=_params(("arbitrary", "arbitrary")),
        name="sb_sample_older",
    )
    dead = dead[:, 0, 0]
    return lax.cond(jnp.all(dead > 0),
                    lambda: acc.astype(BF16),
                    lambda: older(page_table, dead, q_rows, acc, carry, *([cache_k] * pps), *([cache_v] * pps)))


def _layer_a(xp, xs, batch, seq, nseq, t, cache_k, cache_v, layer, state_ssm, state_conv, page_table,
             g_mix, g_ffn, w_in, w_conv, b_conv, dt_bias, a_log, d_skip, g_ssd, w_out, ffn_w):
    w_main = jnp.concatenate([w_in[:, :A_XBC], w_in[:, A_DT:]], axis=1).astype(BF16)
    w_dt = jnp.pad(w_in[:, A_XBC:A_DT], ((0, 0), (0, LANES - SSD_HEADS))).astype(BF16)
    w_out = w_out.astype(BF16)
    gn2 = 2 * SSD_GROUPS * SSD_STATE
    qcol = A_XBC
    kcol = qcol + SB_HEADS * SB_HEAD_DIM
    vcol = kcol + SB_KV_HEADS * SB_HEAD_DIM

    pp = norm_matmul(xp, g_mix, w_main)
    dp = norm_matmul(xp, g_mix, w_dt)
    y_p, ssm_p = ssd_prompt(pp, dp, batch, seq, w_conv, b_conv, dt_bias, a_log, d_skip, g_ssd)
    o_p = sb_prompt(pp, batch, seq)
    xp = out_proj(y_p, o_p, w_out, xp)
    xp = ffn(xp, g_ffn, *ffn_w)
    sbk_p = pp[:, kcol:vcol].reshape(batch, seq, SB_KV_HEADS, SB_HEAD_DIM)
    sbv_p = pp[:, vcol:].reshape(batch, seq, SB_KV_HEADS, SB_HEAD_DIM)
    conv_p = pp.reshape(batch, seq, -1)[:, seq - (SSD_CONV - 1):, A_Z:A_Z + SSD_CONV_DIM]
    ssm_p = ssm_p.reshape(batch, SSD_HEADS, SSD_HEAD_DIM, SSD_STATE)

    ps = norm_matmul(xs, g_mix, w_main)
    ds = norm_matmul(xs, g_mix, w_dt)
    y_s, conv_s, ssm_s = ssd_sample(ps, ds, nseq, t, state_conv, state_ssm.reshape(nseq, SSD_D_INNER, SSD_STATE),
                                    w_conv, b_conv, dt_bias, a_log, d_skip, g_ssd)
    q_rows = ps[:, qcol:kcol].reshape(nseq, t, SB_KV_HEADS, SB_Q_PER_KV, SB_HEAD_DIM)
    q_rows = q_rows.transpose(0, 2, 3, 1, 4).reshape(nseq, SB_HEADS * t, SB_HEAD_DIM)
    o_s = sb_sample(q_rows, ps, cache_k, cache_v, layer, page_table, t)
    o_s = o_s.reshape(nseq, SB_KV_HEADS, SB_Q_PER_KV, t, SB_HEAD_DIM).transpose(0, 3, 1, 2, 4)
    o_s = o_s.reshape(nseq * t, SB_HEADS * SB_HEAD_DIM)
    xs = out_proj(y_s, o_s, w_out, xs)
    xs = ffn(xs, g_ffn, *ffn_w)
    sbk_s = ps[:, kcol:vcol].reshape(nseq, t, SB_KV_HEADS, SB_HEAD_DIM)
    sbv_s = ps[:, vcol:].reshape(nseq, t, SB_KV_HEADS, SB_HEAD_DIM)
    ssm_s = ssm_s.reshape(nseq, SSD_HEADS, SSD_HEAD_DIM, SSD_STATE)
    return xp, xs, (sbk_p, sbv_p, ssm_p, conv_p), (sbk_s, sbv_s, ssm_s, conv_s)


MLA_PREP_ROWS = 256
C_MAIN = 2 * CMLP_DIM + MLA_Q_RANK + MLA_KV_RANK


def _half_head_ones():
    a = lax.broadcasted_iota(jnp.int32, (LANES, LANES), 0) // MLA_ROPE
    b = lax.broadcasted_iota(jnp.int32, (LANES, LANES), 1) // MLA_ROPE
    return (a == b).astype(BF16)


def _rope_norm(x, g, cos, sin, ones64):
    ms = _dot01_right(x * x, ones64) * (1.0 / MLA_ROPE)
    xn = x * lax.rsqrt(ms + EPS) * g
    lane = lax.broadcasted_iota(jnp.int32, x.shape, 1) % MLA_ROPE
    rot = jnp.where(lane < MLA_ROPE // 2, -pltpu.roll(xn, LANES - MLA_ROPE // 2, 1), pltpu.roll(xn, MLA_ROPE // 2, 1))
    return xn * cos + rot * sin


def _mla_prep_kernel(cq_ref, ckv_ref, kpe_ref, cos_ref, sin_ref, gcq_ref, gckv_ref, gqn_ref, gqr_ref, gkr_ref,
                     wqn_ref, wqr_ref, *rest, with_kv):
    if with_kv:
        gkn_ref, wuk_ref, wuv_ref, qn_ref, qp_ref, ckvn_ref, kper_ref, kn_ref, vv_ref, kp2_ref = rest
    else:
        qn_ref, qp_ref, ckvn_ref, kper_ref = rest
    ones64 = _half_head_ones()
    cos, sin = cos_ref[...], sin_ref[...]
    cqn = _rms(cq_ref[...], gcq_ref[...]).astype(BF16)
    qn = _dot(cqn, wqn_ref[...])
    qp = _dot(cqn, wqr_ref[...])
    for h in range(MLA_HEADS):
        cs = slice(h * MLA_NOPE, (h + 1) * MLA_NOPE)
        qn_ref[:, cs] = (_rms(qn[:, cs], gqn_ref[:, cs]) * MLA_SCALE).astype(qn_ref.dtype)
    for b in range(MLA_HEADS * MLA_ROPE // LANES):
        cs = slice(b * LANES, (b + 1) * LANES)
        qp_ref[:, cs] = (_rope_norm(qp[:, cs], gqr_ref[...], cos, sin, ones64) * MLA_SCALE).astype(qp_ref.dtype)
    ckvn = _rms(ckv_ref[...], gckv_ref[...])
    ckvn_ref[...] = ckvn
    kper = _rope_norm(kpe_ref[...], gkr_ref[...], cos, sin, ones64)
    kper_ref[...] = kper
    if with_kv:
        cb = ckvn.astype(BF16)
        kraw = _dot(cb, wuk_ref[...])
        for h in range(MLA_HEADS):
            cs = slice(h * MLA_NOPE, (h + 1) * MLA_NOPE)
            kn_ref[:, cs] = _rms(kraw[:, cs], gkn_ref[...]).astype(kn_ref.dtype)
        vv_ref[...] = _dot(cb, wuv_ref[...]).astype(vv_ref.dtype)
        kp2_ref[...] = (kper + pltpu.roll(kper, MLA_ROPE, 1)).astype(kp2_ref.dtype)


def mla_prep(proj, kpe_raw, cos, sin, g_cq, g_ckv, g_qn_row, g_qr, g_kr, wq_nope, wq_rope, kv_weights=None):
    m = proj.shape[0]
    tr = min(MLA_PREP_ROWS, m)
    ncos = cos.shape[0] // tr
    c0 = 2 * CMLP_DIM // MLA_Q_RANK
    hq = MLA_HEADS * MLA_NOPE
    hr = MLA_HEADS * MLA_ROPE
    with_kv = kv_weights is not None
    const = lambda shape: pl.BlockSpec(shape, lambda i: (0, 0))
    row = lambda w: pl.BlockSpec((tr, w), lambda i: (i, 0))
    pad_row = lambda v: jnp.concatenate([v, jnp.zeros((LANES - v.shape[0],), F32)]).reshape(1, LANES)
    in_specs = [pl.BlockSpec((tr, MLA_Q_RANK), lambda i: (i, c0)),
                pl.BlockSpec((tr, MLA_KV_RANK), lambda i: (i, c0 + 1)),
                row(LANES),
                pl.BlockSpec((tr, LANES), lambda i: (i % ncos, 0)),
                pl.BlockSpec((tr, LANES), lambda i: (i % ncos, 0)),
                const((1, MLA_Q_RANK)), const((1, MLA_KV_RANK)), const((1, hq)), const((1, LANES)), const((1, LANES)),
                const((MLA_Q_RANK, hq)), const((MLA_Q_RANK, hr))]
    args = [proj, proj, kpe_raw, cos, sin, g_cq.reshape(1, -1), g_ckv.reshape(1, -1), g_qn_row.reshape(1, hq),
            jnp.tile(g_qr, LANES // MLA_ROPE).reshape(1, LANES), pad_row(g_kr), wq_nope, wq_rope]
    out_specs = [row(hq), row(hr), row(MLA_KV_RANK), row(LANES)]
    out_shape = [jax.ShapeDtypeStruct((m, hq), BF16), jax.ShapeDtypeStruct((m, hr), BF16),
                 jax.ShapeDtypeStruct((m, MLA_KV_RANK), F32), jax.ShapeDtypeStruct((m, LANES), F32)]
    if with_kv:
        g_kn, w_uk, w_uv = kv_weights
        in_specs += [const((1, MLA_NOPE)), const((MLA_KV_RANK, hq)), const((MLA_KV_RANK, MLA_HEADS * MLA_V))]
        args += [g_kn.reshape(1, -1), w_uk, w_uv]
        out_specs += [row(hq), row(MLA_HEADS * MLA_V), row(LANES)]
        out_shape += [jax.ShapeDtypeStruct((m, hq), BF16), jax.ShapeDtypeStruct((m, MLA_HEADS * MLA_V), BF16),
                      jax.ShapeDtypeStruct((m, LANES), BF16)]
    return pl.pallas_call(
        functools.partial(_mla_prep_kernel, with_kv=with_kv),
        grid=(m // tr,),
        in_specs=in_specs, out_specs=out_specs, out_shape=out_shape,
        compiler_params=_params(("parallel",)),
        name="mla_prep",
    )(*args)


def _softmax_update(s, v, m_ref, l_ref, acc_ref):
    p, alpha = _softmax_stats(s, m_ref, l_ref)
    _softmax_accumulate(p, alpha, v, acc_ref)


def _softmax_stats(s, m_ref, l_ref):
    m_old = m_ref[...]
    m_new = jnp.maximum(m_old, jnp.max(s, axis=-1, keepdims=True))
    alpha = jnp.exp(m_old - m_new)
    p = jnp.exp(s - m_new[:, :1])
    l_ref[...] = alpha * l_ref[...] + jnp.sum(p, axis=-1, keepdims=True)
    m_ref[...] = m_new
    return p.astype(BF16), alpha[:, :1]


def _softmax_accumulate(p, alpha, v, acc_ref):
    acc_ref[...] = alpha * acc_ref[...] + _dot(p, v)


def _mla_prompt_kernel(qn_ref, qp_ref, kn_ref, kp_ref, vv_ref, o_ref, m_s, l_s, acc_s, *, tq, nh):
    qi = pl.program_id(2)
    lane_head = lax.broadcasted_iota(jnp.int32, (tq, LANES), 1) // MLA_ROPE
    row = lax.broadcasted_iota(jnp.int32, (tq, tq), 0)
    col = lax.broadcasted_iota(jnp.int32, (tq, tq), 1)
    m_s[...] = jnp.full(m_s.shape, -jnp.inf, F32)
    l_s[...] = jnp.zeros(l_s.shape, F32)
    acc_s[...] = jnp.zeros(acc_s.shape, F32)

    def process(j, mask):
        ks = pl.ds(pl.multiple_of(j * tq, tq), tq)
        kp = kp_ref[ks, :]
        head_cols = [slice(hh * MLA_NOPE, (hh + 1) * MLA_NOPE) for hh in range(nh)]
        ones = jnp.ones((tq, LANES), BF16)
        ss = []
        for hh, cs in enumerate(head_cols):
            qp = qp_ref[:, (hh // 2) * LANES:(hh // 2 + 1) * LANES]
            q2 = jnp.where(lane_head == hh % 2, qp, jnp.zeros((), qp.dtype))
            s = _dot_nt(jnp.concatenate([qn_ref[:, cs], q2], axis=1),
                        jnp.concatenate([kn_ref[ks, cs], kp], axis=1))
            ss.append(s if mask is None else jnp.where(mask, s, -jnp.inf))
        pa = []
        for hh, s in enumerate(ss):
            m_old = m_s[hh]
            m_new = jnp.maximum(m_old, jnp.max(s, axis=-1, keepdims=True))
            m_s[hh] = m_new
            pa.append((jnp.exp(s - m_new[:, :1]).astype(BF16), jnp.exp(m_old - m_new)))
        for hh, cs in enumerate(head_cols):
            pv = _dot(pa[hh][0], jnp.concatenate([vv_ref[ks, cs], ones], axis=1))
            acc_s[hh] = pa[hh][1] * acc_s[hh] + pv[:, :MLA_V]
            l_s[hh] = pa[hh][1] * l_s[hh] + pv[:, MLA_V:]

    def body(j, _):
        process(j, None)
        return 0

    lax.fori_loop(0, qi, body, 0)
    process(qi, col <= row)
    for hh in range(nh):
        cs = slice(hh * MLA_NOPE, (hh + 1) * MLA_NOPE)
        o_ref[:, cs] = (acc_s[hh] / l_s[hh][:, :1]).astype(o_ref.dtype)


def mla_prompt(qn, qp, kn, kp2, vv, batch, seq, tq=256, nh=4):
    tq = min(tq, seq)
    nq = seq // tq
    hw = nh * MLA_NOPE
    rw = nh * MLA_ROPE
    return pl.pallas_call(
        functools.partial(_mla_prompt_kernel, tq=tq, nh=nh),
        grid=(batch, MLA_HEADS // nh, nq),
        in_specs=[pl.BlockSpec((tq, hw), lambda b, h, i: (b * nq + i, h)),
                  pl.BlockSpec((tq, rw), lambda b, h, i: (b * nq + i, h)),
                  pl.BlockSpec((seq, hw), lambda b, h, i: (b, h)),
                  pl.BlockSpec((seq, LANES), lambda b, h, i: (b, 0)),
                  pl.BlockSpec((seq, hw), lambda b, h, i: (b, h))],
        out_specs=pl.BlockSpec((tq, hw), lambda b, h, i: (b * nq + i, h)),
        out_shape=jax.ShapeDtypeStruct((batch * seq, MLA_HEADS * MLA_V), BF16),
        scratch_shapes=[pltpu.VMEM((nh, tq, LANES), F32), pltpu.VMEM((nh, tq, LANES), F32),
                        pltpu.VMEM((nh, tq, MLA_V), F32)],
        compiler_params=_params(("parallel", "parallel", "arbitrary")),
        name="mla_prompt",
    )(qn, qp, kn, kp2, vv)


def _head_matmul_kernel(a_ref, w_ref, o_ref):
    a = a_ref[...]
    a = a.reshape(-1, a.shape[-1]).astype(BF16)
    o_ref[...] = _dot(a, w_ref[0]).reshape(o_ref.shape).astype(o_ref.dtype)


def absorb_queries(qg, w_uk_t, nseq, t):
    return pl.pallas_call(
        _head_matmul_kernel,
        grid=(MLA_HEADS,),
        in_specs=[pl.BlockSpec((nseq * t, MLA_NOPE), lambda h: (0, h)),
                  pl.BlockSpec((1, MLA_NOPE, MLA_KV_RANK), lambda h: (h, 0, 0))],
        out_specs=pl.BlockSpec((nseq, None, t, MLA_KV_RANK), lambda h: (0, h, 0, 0)),
        out_shape=jax.ShapeDtypeStruct((nseq, MLA_HEADS, t, MLA_KV_RANK), F32),
        compiler_params=_params(("parallel",)),
        name="absorb_queries",
    )(qg, w_uk_t)


def project_latent_out(o_lat, w_uv_h, nseq, t):
    return pl.pallas_call(
        _head_matmul_kernel,
        grid=(MLA_HEADS,),
        in_specs=[pl.BlockSpec((nseq, None, t, MLA_KV_RANK), lambda h: (0, h, 0, 0)),
                  pl.BlockSpec((1, MLA_KV_RANK, MLA_V), lambda h: (h, 0, 0))],
        out_specs=pl.BlockSpec((nseq * t, MLA_V), lambda h: (0, h)),
        out_shape=jax.ShapeDtypeStruct((nseq * t, MLA_HEADS * MLA_V), BF16),
        compiler_params=_params(("parallel",)),
        name="project_latent_out",
    )(o_lat, w_uv_h)


MLA_BLOCK_PAGES = 8


def _mla_sample_kernel(pt_ref, qa_ref, qp_ref, cn_ref, kn_ref, wuk_ref, *refs, pages_per_step, t):
    c_refs = refs[:pages_per_step]
    k_refs = refs[pages_per_step:2 * pages_per_step]
    o_ref, lhs, m_s, l_s, acc_s, cpad, kpad = refs[2 * pages_per_step:]
    b = pl.program_id(0)
    p = pl.program_id(1)
    nk = MLA_HEADS * MLA_NOPE
    rows = MLA_HEADS * t

    @pl.when(jnp.logical_and(b == 0, p == 0))
    def _():
        lhs[0:nk, :] = wuk_ref[...]

    def front(c):
        keys = c.shape[0]
        kt = _dot_nt(lhs[...], c)
        k2 = kt[0:nk, :]
        k2 = (k2 * k2).reshape(MLA_HEADS, MLA_NOPE // 8, 8, keys).sum(axis=1)
        for sh in (4, 2, 1):
            k2 = k2 + pltpu.roll(k2, sh, 1)
        rs = lax.rsqrt(k2 * (1.0 / MLA_NOPE) + EPS).reshape(rows, keys)
        return kt[nk:nk + rows, :] * rs

    def back(s, c, kpe_t, mask):
        s = s + _dot(qp_ref[0].astype(BF16), kpe_t)
        if mask is not None:
            s = jnp.where(mask, s, -jnp.inf)
        _softmax_update(s, c, m_s, l_s, acc_s)

    @pl.when(p == 0)
    def _():
        lhs[nk:nk + rows, :] = qa_ref[0].reshape(rows, MLA_KV_RANK).astype(BF16)
        m_s[...] = jnp.full(m_s.shape, -jnp.inf, F32)
        l_s[...] = jnp.zeros(l_s.shape, F32)
        acc_s[...] = jnp.zeros(acc_s.shape, F32)
        cpad[...] = jnp.zeros(cpad.shape, F32)
        kpad[...] = jnp.zeros(kpad.shape, F32)
        cpad[0:t, :] = cn_ref[...]
        kpad[0:t, :] = kn_ref[...]
        row_t = lax.broadcasted_iota(jnp.int32, (rows, LANES), 0) % t
        col = lax.broadcasted_iota(jnp.int32, (rows, LANES), 1)
        cn = cpad[...].astype(BF16)
        back(front(cn), cn, kpad[...].T[0:MLA_ROPE, :].astype(BF16), col <= row_t)

    bp = min(MLA_BLOCK_PAGES, pages_per_step)
    latent = lambda i: jnp.concatenate([c_refs[i + k][...] for k in range(bp)], axis=0).astype(BF16)
    rotary = lambda i: jnp.concatenate([k_refs[i + k][...] for k in range(bp)], axis=1).astype(BF16)
    starts = list(range(0, pages_per_step, bp))
    pending = front(latent(starts[0]))
    for n, i in enumerate(starts):
        nxt = front(latent(starts[n + 1])) if n + 1 < len(starts) else None
        back(pending, latent(i), rotary(i), None)
        pending = nxt

    @pl.when(p == pl.num_programs(1) - 1)
    def _():
        o_ref[0] = (acc_s[...] / l_s[:, :1]).reshape(o_ref.shape[1:])


def mla_sample(q_abs, q_pe, ckv_new, kpe_new, w_uk_t2, cache_ckv, cache_kpe, layer, page_table, t, pages_per_step=32):
    nseq, n_pages = page_table.shape
    pps = min(pages_per_step, n_pages)
    nsteps = n_pages // pps
    rows = MLA_HEADS * t
    nk = MLA_HEADS * MLA_NOPE
    cache_kpe_t = jnp.swapaxes(cache_kpe, 2, 3)

    def page_spec(shape, i):
        return pl.BlockSpec((None, None) + shape, lambda b, p, pt: (layer, pt[b, p * pps + i], 0, 0))

    grid_spec = pltpu.PrefetchScalarGridSpec(
        num_scalar_prefetch=1,
        grid=(nseq, nsteps),
        in_specs=[pl.BlockSpec((1, MLA_HEADS, t, MLA_KV_RANK), lambda b, p, pt: (b, 0, 0, 0)),
                  pl.BlockSpec((1, rows, MLA_ROPE), lambda b, p, pt: (b, 0, 0)),
                  pl.BlockSpec((t, MLA_KV_RANK), lambda b, p, pt: (b, 0)),
                  pl.BlockSpec((t, LANES), lambda b, p, pt: (b, 0)),
                  pl.BlockSpec((nk, MLA_KV_RANK), lambda b, p, pt: (0, 0))]
                 + [page_spec((PAGE_SIZE, MLA_KV_RANK), i) for i in range(pps)]
                 + [page_spec((MLA_ROPE, PAGE_SIZE), i) for i in range(pps)],
        out_specs=pl.BlockSpec((1, MLA_HEADS, t, MLA_KV_RANK), lambda b, p, pt: (b, 0, 0, 0)),
        scratch_shapes=[pltpu.VMEM((nk + rows, MLA_KV_RANK), BF16),
                        pltpu.VMEM((rows, LANES), F32), pltpu.VMEM((rows, LANES), F32),
                        pltpu.VMEM((rows, MLA_KV_RANK), F32),
                        pltpu.VMEM((PAGE_SIZE, MLA_KV_RANK), F32), pltpu.VMEM((PAGE_SIZE, LANES), F32)],
    )
    return pl.pallas_call(
        functools.partial(_mla_sample_kernel, pages_per_step=pps, t=t),
        grid_spec=grid_spec,
        out_shape=jax.ShapeDtypeStruct((nseq, MLA_HEADS, t, MLA_KV_RANK), F32),
        compiler_params=_params(("arbitrary", "arbitrary")),
        name="mla_sample",
    )(page_table, q_abs, q_pe, ckv_new, kpe_new, w_uk_t2, *([cache_ckv] * pps), *([cache_kpe_t] * pps))


def _gelu(x):
    return 0.5 * x * (1.0 + jnp.tanh(0.7978845608028654 * (x + 0.044715 * (x * x * x))))


def _cmlp_kernel(u_ref, v_ref, gv_ref, bv_ref, w_ref, bs_ref, o_ref, *rest, seg):
    r = u_ref.shape[0]
    u = _gelu(u_ref[...])
    v = _gelu(v_ref[...])
    mu = jnp.mean(v, axis=-1, keepdims=True)
    vc = v - mu
    var = jnp.mean(vc * vc, axis=-1, keepdims=True)
    vn = vc * lax.rsqrt(var + EPS) * gv_ref[...] + bv_ref[...]
    if rest:
        rest[0][...] = vn
    vb = vn.astype(BF16)
    row = lax.broadcasted_iota(jnp.int32, (r, r), 0)
    col = lax.broadcasted_iota(jnp.int32, (r, r), 1)
    keep = jnp.logical_and(col <= row, (row // seg) == (col // seg))
    gw = CMLP_DIM // CMLP_GROUPS
    for g in range(CMLP_GROUPS):
        cs = slice(g * gw, (g + 1) * gw)
        w = jnp.where(keep, w_ref[g], 0.0).astype(BF16)
        mixed = _dot(w, vb[:, cs]) + bs_ref[:, g:g + 1]
        o_ref[:, cs] = (u[:, cs] * mixed).astype(o_ref.dtype)


def chunk_mlp(proj, g_v, b_v, w_mix, b_mix, seg, emit_v):
    m = proj.shape[0]
    d = CMLP_DIM
    const2 = lambda shape: pl.BlockSpec(shape, lambda i: (0, 0))
    out_specs = [pl.BlockSpec((ROW_TILE, d), lambda i: (i, 0))]
    out_shape = [jax.ShapeDtypeStruct((m, d), BF16)]
    if emit_v:
        out_specs.append(pl.BlockSpec((ROW_TILE, d), lambda i: (i, 0)))
        out_shape.append(jax.ShapeDtypeStruct((m, d), F32))
    return pl.pallas_call(
        functools.partial(_cmlp_kernel, seg=seg),
        grid=(m // ROW_TILE,),
        in_specs=[pl.BlockSpec((ROW_TILE, d), lambda i: (i, 0)),
                  pl.BlockSpec((ROW_TILE, d), lambda i: (i, 1)),
                  const2((1, d)), const2((1, d)),
                  pl.BlockSpec((CMLP_GROUPS, ROW_TILE, ROW_TILE), lambda i: (0, 0, 0)),
                  const2((ROW_TILE, CMLP_GROUPS))],
        out_specs=out_specs, out_shape=out_shape,
        compiler_params=_params(("parallel",)),
        name="chunk_mlp",
    )(proj, proj, g_v.reshape(1, d), b_v.reshape(1, d), w_mix, b_mix)


def _rope_tables(pos):
    half = MLA_ROPE // 2
    inv = ROPE_THETA ** (-jnp.arange(half, dtype=F32) / half)
    ang = pos.astype(F32)[:, None] * inv[None, :]
    reps = LANES // half
    return jnp.tile(jnp.cos(ang), (1, reps)), jnp.tile(jnp.sin(ang), (1, reps))


def _layer_c(xp, xs, batch, seq, nseq, t, cache_ckv, cache_kpe, layer, page_table, g_mix, g_ffn, w_in, g_cq, g_ckv,
             w_uq, g_qn, g_qr, g_kn, g_kr, w_uk, w_uv, g_v, b_v, w_s, b_s, w_out, ffn_w):
    past = page_table.shape[1] * PAGE_SIZE
    w_main = jnp.concatenate([w_in[:, C_KPE:], w_in[:, :C_CKV]], axis=1).astype(BF16)
    w_kpe = jnp.pad(w_in[:, C_CKV:C_KPE], ((0, 0), (0, LANES - MLA_ROPE))).astype(BF16)
    wq = w_uq.reshape(MLA_Q_RANK, MLA_HEADS, MLA_NOPE + MLA_ROPE)
    wq_nope = wq[:, :, :MLA_NOPE].reshape(MLA_Q_RANK, -1).astype(BF16)
    wq_rope = wq[:, :, MLA_NOPE:].reshape(MLA_Q_RANK, -1).astype(BF16)
    w_uk2 = w_uk.reshape(MLA_KV_RANK, -1).astype(BF16)
    w_uv2 = w_uv.reshape(MLA_KV_RANK, -1).astype(BF16)
    w_uk_t = w_uk.transpose(1, 2, 0).astype(BF16)
    w_uv_h = w_uv.transpose(1, 0, 2).astype(BF16)
    w_out = w_out.astype(BF16)
    ones_row = jnp.ones((MLA_HEADS * MLA_NOPE,), F32)

    pp = norm_matmul(xp, g_mix, w_main)
    kp = norm_matmul(xp, g_mix, w_kpe)
    cos_p, sin_p = _rope_tables(jnp.arange(seq))
    qn, qp, ckv_p, kpe_p, kn, vv, kp2 = mla_prep(pp, kp, cos_p, sin_p, g_cq, g_ckv, jnp.tile(g_qn, MLA_HEADS) * ones_row,
                                                 g_qr, g_kr, wq_nope, wq_rope, (g_kn, w_uk2, w_uv2))
    o_mla_p = mla_prompt(qn, qp, kn, kp2, vv, batch, seq)
    o_cm_p = chunk_mlp(pp, g_v, b_v, w_s, b_s.T, CMLP_CHUNK, False)[0]
    xp = out_proj(o_mla_p, o_cm_p, w_out, xp)
    xp = ffn(xp, g_ffn, *ffn_w)
    ckv_p = ckv_p.reshape(batch, seq, MLA_KV_RANK)
    kpe_p = kpe_p[:, :MLA_ROPE].reshape(batch, seq, MLA_ROPE)

    ps = norm_matmul(xs, g_mix, w_main)
    ks = norm_matmul(xs, g_mix, w_kpe)
    rows_s = nseq * t
    tr = min(MLA_PREP_ROWS, rows_s)
    cos_s, sin_s = _rope_tables(past + (jnp.arange(tr) % t))
    qg, qr, ckv_s, kpe_s = mla_prep(ps, ks, cos_s, sin_s, g_cq, g_ckv, jnp.tile(g_qn * g_kn, MLA_HEADS), g_qr, g_kr,
                                    wq_nope, wq_rope)
    q_abs = absorb_queries(qg, w_uk_t, nseq, t)
    q_pe = qr.reshape(nseq, t, MLA_HEADS, MLA_ROPE).transpose(0, 2, 1, 3).reshape(nseq, MLA_HEADS * t, MLA_ROPE)
    o_lat = mla_sample(q_abs, q_pe.astype(F32), ckv_s, kpe_s, w_uk2.T, cache_ckv, cache_kpe, layer, page_table, t)
    o_mla_s = project_latent_out(o_lat, w_uv_h, nseq, t)
    reps = ROW_TILE // t
    w_mix_s = jnp.tile(w_s[:, :t, :t], (1, reps, reps))
    b_mix_s = jnp.tile(b_s[:, :t].T, (reps, 1))
    o_cm_s, cv_s = chunk_mlp(ps, g_v, b_v, w_mix_s, b_mix_s, t, True)
    xs = out_proj(o_mla_s, o_cm_s, w_out, xs)
    xs = ffn(xs, g_ffn, *ffn_w)
    ckv_s = ckv_s.reshape(nseq, t, MLA_KV_RANK)
    kpe_s = kpe_s[:, :MLA_ROPE].reshape(nseq, t, MLA_ROPE)
    return xp, xs, (ckv_p, kpe_p), (ckv_s, kpe_s, cv_s.reshape(nseq, t, CMLP_DIM))


def kernel(x_prompt, x_sample, cache_sb_k, cache_sb_v, cache_mla_ckv, cache_mla_kpe, state_ssm, state_conv, page_table, norm_mix, norm_ffn, w_in_a, w_conv, b_conv, dt_bias, a_log, d_skip, g_ssd, w_out_a, w_in_c, g_cq, g_ckv, w_uq, g_qn, g_qr, g_kn, g_kr, w_uk, w_uv, g_v, b_v, w_s, b_s, w_out_c, w_ffn1, w_ffn2):
    batch, seq, dm = x_prompt.shape
    nseq, t, _ = x_sample.shape
    depth = norm_mix.shape[0]
    xp = x_prompt.reshape(batch * seq, dm)
    xs = x_sample.reshape(nseq * t, dm)
    pa, sa, pc, sc = [], [], [], []
    w1_all = w_ffn1.astype(BF16)
    w2_all = w_ffn2.astype(BF16)
    for l in range(depth):
        i = l // 2
        ffn_w = (w1_all, w2_all, l)
        if l % 2 == 0:
            xp, xs, op, os_ = _layer_a(xp, xs, batch, seq, nseq, t, cache_sb_k, cache_sb_v, i, state_ssm[i], state_conv[i],
                                       page_table, norm_mix[l], norm_ffn[l], w_in_a[i], w_conv[i], b_conv[i], dt_bias[i],
                                       a_log[i], d_skip[i], g_ssd[i], w_out_a[i], ffn_w)
            pa.append(op)
            sa.append(os_)
        else:
            xp, xs, op, os_ = _layer_c(xp, xs, batch, seq, nseq, t, cache_mla_ckv, cache_mla_kpe, i, page_table,
                                       norm_mix[l], norm_ffn[l], w_in_c[i], g_cq[i], g_ckv[i], w_uq[i], g_qn[i], g_qr[i],
                                       g_kn[i], g_kr[i], w_uk[i], w_uv[i], g_v[i], b_v[i], w_s[i], b_s[i], w_out_c[i],
                                       ffn_w)
            pc.append(op)
            sc.append(os_)
    stack = lambda items, k: jnp.stack([it[k] for it in items])
    return (xp.reshape(batch, seq, dm), xs.reshape(nseq, t, dm),
            stack(pa, 0), stack(pa, 1), stack(pa, 2), stack(pa, 3), stack(pc, 0), stack(pc, 1),
            stack(sa, 0), stack(sa, 1), stack(sa, 2), stack(sa, 3), stack(sc, 0), stack(sc, 1), stack(sc, 2))
```

```python
import functools

import jax
import jax.numpy as jnp
from jax import lax
from jax.experimental import pallas as pl
from jax.experimental.pallas import tpu as pltpu

F32 = jnp.float32
BF16 = jnp.bfloat16

D_MODEL = 2048
PAGE_SIZE = 128
SSD_D_INNER = D_MODEL
SSD_HEAD_DIM = 64
SSD_HEADS = SSD_D_INNER // SSD_HEAD_DIM
SSD_GROUPS = 8
SSD_STATE = 128
SSD_CONV = 4
SSD_CHUNK = 128
SSD_CONV_DIM = SSD_D_INNER + 2 * SSD_GROUPS * SSD_STATE
SB_HEAD_DIM = 128
SB_HEADS = D_MODEL // SB_HEAD_DIM
SB_KV_HEADS = 4
SB_Q_PER_KV = SB_HEADS // SB_KV_HEADS
SB_SCALE = SB_HEAD_DIM ** -0.5
MLA_HEADS = 16
MLA_Q_RANK = 512
MLA_KV_RANK = 512
MLA_NOPE = 128
MLA_ROPE = 64
MLA_V = 128
MLA_SCALE = (MLA_NOPE + MLA_ROPE) ** -0.5
ROPE_THETA = 10000.0
CMLP_DIM = D_MODEL
CMLP_GROUPS = 8
CMLP_CHUNK = 128
FFN_DIM = 4 * D_MODEL
EPS = 1e-6

A_Z = SSD_D_INNER
A_XBC = A_Z + SSD_CONV_DIM
A_DT = A_XBC + SSD_HEADS
A_Q = A_DT + SB_HEADS * SB_HEAD_DIM
A_K = A_Q + SB_KV_HEADS * SB_HEAD_DIM
C_CQ = MLA_Q_RANK
C_CKV = C_CQ + MLA_KV_RANK
C_KPE = C_CKV + MLA_ROPE
C_U = C_KPE + CMLP_DIM

LANES = 128
ROW_TILE = 128
VMEM_LIMIT = 56 * 1024 * 1024


def _params(semantics, vmem=VMEM_LIMIT, flags=None):
    return pltpu.CompilerParams(dimension_semantics=semantics, vmem_limit_bytes=vmem, flags=flags)


def _split3(x):
    hi = x.astype(BF16)
    r1 = x - hi.astype(F32)
    mid = r1.astype(BF16)
    lo = (r1 - mid.astype(F32)).astype(BF16)
    return hi, mid, lo


def _dot(a, b):
    return jnp.dot(a, b, preferred_element_type=F32)


def _dot_nt(a, b):
    return lax.dot_general(a, b, (((1,), (1,)), ((), ())), preferred_element_type=F32)


def _dot01_right(x, m01):
    hi, mid, lo = _split3(x)
    return _dot(hi, m01) + _dot(mid, m01) + _dot(lo, m01)


def _dot01_left(m01, x):
    hi, mid, lo = _split3(x)
    return _dot(m01, hi) + _dot(m01, mid) + _dot(m01, lo)


def _softplus(x):
    return jnp.maximum(x, 0.0) + jnp.log1p(jnp.exp(-jnp.abs(x)))


def _silu(x):
    return x * jax.nn.sigmoid(x)


def _rms(x, g):
    ms = jnp.mean(x * x, axis=-1, keepdims=True)
    return x * lax.rsqrt(ms + EPS) * g


def _norm_matmul_kernel(x_ref, g_ref, w_ref, o_ref, h_ref):
    @pl.when(pl.program_id(1) == 0)
    def _():
        h_ref[...] = _rms(x_ref[...], g_ref[...]).astype(BF16)

    o_ref[...] = _dot(h_ref[...], w_ref[...])


def norm_matmul(x, g, w, tm=1024, tn=512):
    m, k = x.shape
    n = w.shape[1]
    tm, tn = min(tm, m), min(tn, n)
    return pl.pallas_call(
        _norm_matmul_kernel,
        grid=(m // tm, n // tn),
        in_specs=[pl.BlockSpec((tm, k), lambda i, j: (i, 0)),
                  pl.BlockSpec((1, k), lambda i, j: (0, 0)),
                  pl.BlockSpec((k, tn), lambda i, j: (0, j))],
        out_specs=pl.BlockSpec((tm, tn), lambda i, j: (i, j)),
        out_shape=jax.ShapeDtypeStruct((m, n), F32),
        scratch_shapes=[pltpu.VMEM((tm, k), BF16)],
        compiler_params=_params(("parallel", "arbitrary")),
        name="norm_matmul",
    )(x, g.reshape(1, k), w)


def _out_proj_kernel(a_ref, b_ref, wa_ref, wb_ref, r_ref, o_ref):
    o_ref[...] = r_ref[...] + (_dot(a_ref[...], wa_ref[...]) + _dot(b_ref[...], wb_ref[...]))


def out_proj(a, b, w, res, tm=1024, tn=512):
    m, kh = a.shape
    n = w.shape[1]
    tm, tn = min(tm, m), min(tn, n)
    return pl.pallas_call(
        _out_proj_kernel,
        grid=(m // tm, n // tn),
        in_specs=[pl.BlockSpec((tm, kh), lambda i, j: (i, 0)),
                  pl.BlockSpec((tm, kh), lambda i, j: (i, 0)),
                  pl.BlockSpec((kh, tn), lambda i, j: (0, j)),
                  pl.BlockSpec((kh, tn), lambda i, j: (1, j)),
                  pl.BlockSpec((tm, tn), lambda i, j: (i, j))],
        out_specs=pl.BlockSpec((tm, tn), lambda i, j: (i, j)),
        out_shape=jax.ShapeDtypeStruct((m, n), F32),
        compiler_params=_params(("parallel", "arbitrary")),
        name="out_proj",
    )(a, b, w, w, res)


def _ffn_kernel(x_ref, g_ref, w1_ref, w2_ref, o_ref, h_ref):
    @pl.when(pl.program_id(1) == 0)
    def _():
        x = x_ref[...]
        h_ref[...] = _rms(x, g_ref[...]).astype(BF16)
        o_ref[...] = x

    a = jnp.maximum(_dot(h_ref[...], w1_ref[...]), 0.0)
    o_ref[...] += _dot((a * a).astype(BF16), w2_ref[...])


def ffn(x, g, w1, w2, layer, tm=1024, tf=512):
    m, d = x.shape
    f = w1.shape[2]
    tm, tf = min(tm, m), min(tf, f)
    return pl.pallas_call(
        _ffn_kernel,
        grid=(m // tm, f // tf),
        in_specs=[pl.BlockSpec((tm, d), lambda i, j: (i, 0)),
                  pl.BlockSpec((1, d), lambda i, j: (0, 0)),
                  pl.BlockSpec((None, d, tf), lambda i, j: (layer, 0, j)),
                  pl.BlockSpec((None, tf, d), lambda i, j: (layer, j, 0))],
        out_specs=pl.BlockSpec((tm, d), lambda i, j: (i, 0)),
        out_shape=jax.ShapeDtypeStruct((m, d), F32),
        scratch_shapes=[pltpu.VMEM((tm, d), BF16)],
        compiler_params=_params(("parallel", "arbitrary")),
        name="ffn",
    )(x, g.reshape(1, d), w1, w2)


def _head_expand():
    h = lax.broadcasted_iota(jnp.int32, (LANES, SSD_D_INNER), 0)
    c = lax.broadcasted_iota(jnp.int32, (LANES, SSD_D_INNER), 1)
    return (c // SSD_HEAD_DIM == h).astype(BF16)


def _conv_silu(pad_ref, w_ref, b_ref, lead, rows):
    acc = b_ref[...] + w_ref[SSD_CONV - 1:SSD_CONV, :] * pad_ref[lead:lead + rows, :]
    for k in range(SSD_CONV - 1):
        s = lead - (SSD_CONV - 1) + k
        acc = acc + w_ref[k:k + 1, :] * pad_ref[s:s + rows, :]
    return _silu(acc)


def _ssd_intra(xs, bc, z_unused, dt_raw, dtb, alog, e01, same_block):
    r = xs.shape[0]
    row = lax.broadcasted_iota(jnp.int32, (r, r), 0)
    col = lax.broadcasted_iota(jnp.int32, (r, r), 1)
    causal = jnp.logical_and(col <= row, same_block)
    dt = _softplus(dt_raw + dtb)
    dta = dt * (-jnp.exp(alog))
    at = _dot01_left(causal.astype(BF16), dta)
    at_t = at.T
    at_x = _dot01_right(at, e01)
    dt_x = _dot01_right(dt, e01)
    xdt = xs * dt_x
    return dict(causal=causal, at=at, at_t=at_t, at_x=at_x, xdt=xdt)


def _ssd_diag_pair(q, pr, cb, at, at_t, causal, xdt_bf):
    lane = lax.broadcasted_iota(jnp.int32, (q, LANES), 1)
    ys = []
    for h in (2 * pr, 2 * pr + 1):
        d = at[:, h:h + 1] - at_t[h:h + 1, :]
        dec = jnp.exp(jnp.where(causal, d, -jnp.inf))
        ys.append(_dot((cb * dec).astype(BF16), xdt_bf))
    return jnp.where(lane < SSD_HEAD_DIM, ys[0], ys[1])


def _gated_group_norm(y, z, g_ref, o_ref):
    gw = SSD_D_INNER // SSD_GROUPS
    for g in range(SSD_GROUPS):
        cs = slice(g * gw, (g + 1) * gw)
        yg = y[:, cs] * _silu(z[:, cs])
        o_ref[:, cs] = _rms(yg, g_ref[:, cs]).astype(o_ref.dtype)


def _ssd_prompt_kernel(z_ref, xs_ref, bc_ref, dt_ref, wcx_ref, wcb_ref, bcx_ref, bcb_ref, dtb_ref,
                       alog_ref, dsk_ref, gss_ref, e_ref, y_ref, s_ref, padx, padb, st, ybuf):
    c = pl.program_id(1)
    q = ROW_TILE
    lead = 8
    gn = SSD_GROUPS * SSD_STATE

    @pl.when(c == 0)
    def _():
        padx[0:lead, :] = jnp.zeros((lead, SSD_D_INNER), F32)
        padb[0:lead, :] = jnp.zeros((lead, 2 * gn), F32)
        st[...] = jnp.zeros(st.shape, F32)

    padx[lead:lead + q, :] = xs_ref[...]
    padb[lead:lead + q, :] = bc_ref[...]
    xs = _conv_silu(padx, wcx_ref, bcx_ref, lead, q)
    bc = _conv_silu(padb, wcb_ref, bcb_ref, lead, q)
    padx[0:lead, :] = padx[q:q + lead, :]
    padb[0:lead, :] = padb[q:q + lead, :]

    e01 = e_ref[...]
    whole = jnp.full((q, q), True)
    r = _ssd_intra(xs, bc, None, dt_ref[...], dtb_ref[...], alog_ref[...], e01, whole)
    at, at_t, at_x, xdt, causal = r["at"], r["at_t"], r["at_x"], r["xdt"], r["causal"]
    eat = jnp.exp(at_x)
    last = at_x[q - 1:q, :]
    xw = (xdt * jnp.exp(last - at_x)).astype(BF16)
    cd = jnp.exp(last)
    xdt_bf = xdt.astype(BF16)

    for g in range(SSD_GROUPS):
        bm = bc[:, g * SSD_STATE:(g + 1) * SSD_STATE]
        cm = bc[:, gn + g * SSD_STATE:gn + (g + 1) * SSD_STATE].astype(BF16)
        cb = _dot_nt(cm, bm.astype(BF16))
        bm_t = bm.T.astype(BF16)
        for pr in (2 * g, 2 * g + 1):
            cs = slice(pr * LANES, (pr + 1) * LANES)
            yd = _ssd_diag_pair(q, pr, cb, at, at_t, causal, xdt_bf[:, cs])
            s_in = st[:, cs]
            yo = _dot(cm, s_in.astype(BF16)) * eat[:, cs]
            st[:, cs] = s_in * cd[:, cs] + _dot(bm_t, xw[:, cs])
            ybuf[:, cs] = yd + yo

    y = ybuf[...] + dsk_ref[...] * xs
    _gated_group_norm(y, z_ref[...], gss_ref, y_ref)

    @pl.when(c == pl.num_programs(1) - 1)
    def _():
        for pr in range(SSD_D_INNER // LANES):
            cs = slice(pr * LANES, (pr + 1) * LANES)
            s_ref[0, cs, :] = st[:, cs].T


def ssd_prompt(proj, dtp, batch, seq, w_conv, b_conv, dt_bias, a_log, d_skip, g_ssd):
    nc = seq // ROW_TILE
    gn = SSD_GROUPS * SSD_STATE
    d = SSD_D_INNER
    row = lambda b, c: b * nc + c
    const = lambda shape: pl.BlockSpec(shape, lambda b, c: (0, 0))
    pad128 = lambda v: jnp.pad(v, (0, LANES - v.shape[0])).reshape(1, LANES)
    return pl.pallas_call(
        _ssd_prompt_kernel,
        grid=(batch, nc),
        in_specs=[pl.BlockSpec((ROW_TILE, d), lambda b, c: (row(b, c), 0)),
                  pl.BlockSpec((ROW_TILE, d), lambda b, c: (row(b, c), 1)),
                  pl.BlockSpec((ROW_TILE, 2 * gn), lambda b, c: (row(b, c), 2)),
                  pl.BlockSpec((ROW_TILE, LANES), lambda b, c: (row(b, c), 0)),
                  const((SSD_CONV, d)), const((SSD_CONV, 2 * gn)), const((1, d)), const((1, 2 * gn)),
                  const((1, LANES)), const((1, LANES)), const((1, d)), const((1, d)),
                  const((LANES, d))],
        out_specs=[pl.BlockSpec((ROW_TILE, d), lambda b, c: (row(b, c), 0)),
                   pl.BlockSpec((1, d, SSD_STATE), lambda b, c: (b, 0, 0))],
        out_shape=[jax.ShapeDtypeStruct((batch * seq, d), BF16),
                   jax.ShapeDtypeStruct((batch, d, SSD_STATE), F32)],
        scratch_shapes=[pltpu.VMEM((ROW_TILE + 8, d), F32), pltpu.VMEM((ROW_TILE + 8, 2 * gn), F32),
                        pltpu.VMEM((SSD_STATE, d), F32), pltpu.VMEM((ROW_TILE, d), F32)],
        compiler_params=_params(("parallel", "arbitrary")),
        name="ssd_prompt",
    )(proj, proj, proj, dtp, w_conv[:, :d], w_conv[:, d:], b_conv[:d].reshape(1, d),
      b_conv[d:].reshape(1, 2 * gn), pad128(dt_bias), pad128(a_log),
      jnp.repeat(d_skip, SSD_HEAD_DIM).reshape(1, d), g_ssd.reshape(1, d), _head_expand())


def _ssd_sample_kernel(z_ref, xs_ref, bc_ref, dt_ref, cvx_ref, cvb_ref, s0_ref, wcx_ref, wcb_ref,
                       bcx_ref, bcb_ref, dtb_ref, alog_ref, dsk_ref, gss_ref, e_ref, et_ref,
                       y_ref, cox_ref, cob_ref, s_ref,
                       padx, padb, ybuf, yoff, eat_s, cm_s, bm_s, xwt_s, att_s):
    j = pl.program_id(1)
    r = ROW_TILE
    t = xs_ref.shape[0] // cvx_ref.shape[0]
    nseq = r // t
    lead = 8
    gn = SSD_GROUPS * SSD_STATE
    d = SSD_D_INNER
    hist = SSD_CONV - 1

    @pl.when(j == 0)
    def _():
        def conv(pad, raw_ref, cv_ref, w_ref, b_ref, co_ref):
            width = raw_ref.shape[1]
            pad[:, lead - hist:lead, :] = cv_ref[...]
            pad[:, lead:lead + t, :] = raw_ref[...].reshape(nseq, t, width)
            acc = b_ref[...] + w_ref[hist:hist + 1, :] * pad[:, lead:lead + t, :]
            for k in range(hist):
                s = lead - hist + k
                acc = acc + w_ref[k:k + 1, :] * pad[:, s:s + t, :]
            co_ref[...] = pad[:, lead + t - hist:lead + t, :]
            return _silu(acc).reshape(r, width)

        xs = conv(padx, xs_ref, cvx_ref, wcx_ref, bcx_ref, cox_ref)
        bc = conv(padb, bc_ref, cvb_ref, wcb_ref, bcb_ref, cob_ref)
        e01 = e_ref[...]
        row = lax.broadcasted_iota(jnp.int32, (r, r), 0)
        col = lax.broadcasted_iota(jnp.int32, (r, r), 1)
        same = (row // t) == (col // t)
        q = _ssd_intra(xs, bc, None, dt_ref[...], dtb_ref[...], alog_ref[...], e01, same)
        at, at_t, at_x, xdt, causal = q["at"], q["at_t"], q["at_x"], q["xdt"], q["causal"]
        eat_s[...] = jnp.exp(at_x)
        last_sel = jnp.logical_and(same, (col % t) == t - 1).astype(BF16)
        last = _dot01_left(last_sel, at_x)
        xw = xdt * jnp.exp(last - at_x)
        xdt_bf = xdt.astype(BF16)
        att_s[...] = _dot01_left(et_ref[...], at_t)
        for g in range(SSD_GROUPS):
            bm = bc[:, g * SSD_STATE:(g + 1) * SSD_STATE].astype(BF16)
            cm = bc[:, gn + g * SSD_STATE:gn + (g + 1) * SSD_STATE].astype(BF16)
            bm_s[:, g * SSD_STATE:(g + 1) * SSD_STATE] = bm
            cm_s[:, g * SSD_STATE:(g + 1) * SSD_STATE] = cm
            cb = _dot_nt(cm, bm)
            for pr in (2 * g, 2 * g + 1):
                cs = slice(pr * LANES, (pr + 1) * LANES)
                ybuf[:, cs] = _ssd_diag_pair(r, pr, cb, at, at_t, causal, xdt_bf[:, cs])
                xwt_s[cs, :] = xw[:, cs].T.astype(BF16)
        ybuf[...] = ybuf[...] + dsk_ref[...] * xs
        yoff[...] = jnp.zeros(yoff.shape, F32)

    rowseq = lax.broadcasted_iota(jnp.int32, (r, LANES), 0) // t
    laneseq = lax.broadcasted_iota(jnp.int32, (LANES, r), 1) // t
    lastlane = lax.broadcasted_iota(jnp.int32, (LANES, r), 1) == j * t + t - 1
    gw = d // SSD_GROUPS
    for g in range(SSD_GROUPS):
        rs = slice(g * gw, (g + 1) * gw)
        gs = slice(g * SSD_STATE, (g + 1) * SSD_STATE)
        s0 = s0_ref[0, rs, :]
        yo = _dot_nt(cm_s[:, gs], s0.astype(BF16))
        for half in range(gw // LANES):
            cs = slice(g * gw + half * LANES, g * gw + (half + 1) * LANES)
            yoff[:, cs] += jnp.where(rowseq == j, yo[:, half * LANES:(half + 1) * LANES], 0.0)
        sel = jnp.concatenate([laneseq, laneseq], axis=0) == j
        sc = _dot(jnp.where(sel, xwt_s[rs, :], jnp.zeros((), BF16)), bm_s[:, gs])
        lastc = jnp.concatenate([lastlane, lastlane], axis=0)
        cdc = jnp.exp(jnp.sum(jnp.where(lastc, att_s[rs, :], 0.0), axis=1, keepdims=True))
        s_ref[0, rs, :] = s0 * cdc + sc

    @pl.when(j == pl.num_programs(1) - 1)
    def _():
        y = ybuf[...] + yoff[...] * eat_s[...]
        _gated_group_norm(y, z_ref[...], gss_ref, y_ref)


def ssd_sample(proj, dtp, nseq, t, state_conv, state_ssm, w_conv, b_conv, dt_bias, a_log, d_skip, g_ssd):
    spb = ROW_TILE // t
    nb = nseq // spb
    gn = SSD_GROUPS * SSD_STATE
    d = SSD_D_INNER
    hist = SSD_CONV - 1
    const = lambda shape: pl.BlockSpec(shape, lambda i, j: (0, 0))
    pad128 = lambda v: jnp.pad(v, (0, LANES - v.shape[0])).reshape(1, LANES)
    e01 = _head_expand()
    outs = pl.pallas_call(
        _ssd_sample_kernel,
        grid=(nb, spb),
        in_specs=[pl.BlockSpec((ROW_TILE, d), lambda i, j: (i, 0)),
                  pl.BlockSpec((ROW_TILE, d), lambda i, j: (i, 1)),
                  pl.BlockSpec((ROW_TILE, 2 * gn), lambda i, j: (i, 2)),
                  pl.BlockSpec((ROW_TILE, LANES), lambda i, j: (i, 0)),
                  pl.BlockSpec((spb, hist, d), lambda i, j: (i, 0, 0)),
                  pl.BlockSpec((spb, hist, 2 * gn), lambda i, j: (i, 0, 1)),
                  pl.BlockSpec((1, d, SSD_STATE), lambda i, j: (i * spb + j, 0, 0)),
                  const((SSD_CONV, d)), const((SSD_CONV, 2 * gn)), const((1, d)), const((1, 2 * gn)),
                  const((1, LANES)), const((1, LANES)), const((1, d)), const((1, d)),
                  const((LANES, d)), const((d, LANES))],
        out_specs=[pl.BlockSpec((ROW_TILE, d), lambda i, j: (i, 0)),
                   pl.BlockSpec((spb, hist, d), lambda i, j: (i, 0, 0)),
                   pl.BlockSpec((spb, hist, 2 * gn), lambda i, j: (i, 0, 0)),
                   pl.BlockSpec((1, d, SSD_STATE), lambda i, j: (i * spb + j, 0, 0))],
        out_shape=[jax.ShapeDtypeStruct((nseq * t, d), BF16),
                   jax.ShapeDtypeStruct((nseq, hist, d), F32),
                   jax.ShapeDtypeStruct((nseq, hist, 2 * gn), F32),
                   jax.ShapeDtypeStruct((nseq, d, SSD_STATE), F32)],
        scratch_shapes=[pltpu.VMEM((spb, 8 + t, d), F32), pltpu.VMEM((spb, 8 + t, 2 * gn), F32),
                        pltpu.VMEM((ROW_TILE, d), F32), pltpu.VMEM((ROW_TILE, d), F32),
                        pltpu.VMEM((ROW_TILE, d), F32),
                        pltpu.VMEM((ROW_TILE, gn), BF16), pltpu.VMEM((ROW_TILE, gn), BF16),
                        pltpu.VMEM((d, ROW_TILE), BF16), pltpu.VMEM((d, ROW_TILE), F32)],
        compiler_params=_params(("parallel", "arbitrary")),
        name="ssd_sample",
    )(proj, proj, proj, dtp, state_conv, state_conv, state_ssm,
      w_conv[:, :d], w_conv[:, d:], b_conv[:d].reshape(1, d), b_conv[d:].reshape(1, 2 * gn),
      pad128(dt_bias), pad128(a_log), jnp.repeat(d_skip, SSD_HEAD_DIM).reshape(1, d),
      g_ssd.reshape(1, d), e01, e01.T)
    y, cox, cob, s_new = outs
    return y, jnp.concatenate([cox, cob], axis=-1), s_new


SB_DEAD = -120.0


def _later_matrix():
    j = lax.broadcasted_iota(jnp.int32, (2 * LANES, 2 * LANES), 0) % LANES
    s = lax.broadcasted_iota(jnp.int32, (2 * LANES, 2 * LANES), 1)
    return jnp.where(jnp.logical_or(j > s, s >= LANES), -1.0, 0.0).astype(BF16)


def _sb_weights(zs, vis, carry, later01):
    assert vis is None or len(zs) == 1
    rows = zs[0].shape[0]
    z = jnp.concatenate(zs, axis=0)
    sp = jnp.maximum(z, 0.0) + jnp.log(1.0 + jnp.exp(-jnp.abs(z)))
    ls = z - sp
    if vis is not None:
        sp = jnp.where(vis, sp, 0.0)
    hi = sp.astype(BF16)
    lo = (sp - hi.astype(F32)).astype(BF16)
    lc = _dot(jnp.concatenate([hi, lo], axis=1), later01)
    ws = []
    for i in range(len(zs)):
        rs = slice(i * rows, (i + 1) * rows)
        w = jnp.exp(ls[rs] + lc[rs, :LANES] + carry)
        ws.append(w if vis is None else jnp.where(vis, w, 0.0))
        carry = carry + lc[rs, LANES:]
    return ws, carry


def _sb_prompt_kernel(q_ref, k_ref, v_ref, o_ref, acc, carry):
    qi = pl.program_id(2)
    t = ROW_TILE
    g4 = SB_Q_PER_KV
    later01 = _later_matrix()
    q = q_ref[...]
    qs = jnp.concatenate([q[:, g * SB_HEAD_DIM:(g + 1) * SB_HEAD_DIM] for g in range(g4)], axis=0)
    qs = (qs * SB_SCALE).astype(BF16)
    rows = g4 * t
    chunks = 2
    hr = rows // chunks

    def kv_rows(j):
        return pl.ds(j * t, t) if isinstance(j, int) else pl.ds(pl.multiple_of(j * t, t), t)

    def step(js, vis):
        ks = [k_ref[kv_rows(j), :].astype(BF16) for j in js]
        vs = jnp.concatenate([v_ref[kv_rows(j), :].astype(BF16) for j in js], axis=0)
        zs = [[_dot_nt(qs[c * hr:(c + 1) * hr], k) for k in ks] for c in range(chunks)]
        wcat = []
        for c in range(chunks):
            rs = slice(c * hr, (c + 1) * hr)
            ws, cn = _sb_weights(zs[c], vis, carry[rs, :], later01)
            carry[rs, :] = cn
            wcat.append(jnp.concatenate([w.astype(BF16) for w in ws], axis=1))
        for c in range(chunks):
            acc[c * hr:(c + 1) * hr, :] += _dot(wcat[c], vs)

    acc[...] = jnp.zeros(acc.shape, F32)
    carry[...] = jnp.zeros(carry.shape, F32)
    row_t = lax.broadcasted_iota(jnp.int32, (hr, t), 0) % t
    col = lax.broadcasted_iota(jnp.int32, (hr, t), 1)
    step([qi], col < row_t)

    alive = lambda: jnp.max(carry[...]) > SB_DEAD

    def cond(state):
        i, live = state
        return jnp.logical_and(i < qi // 2, live)

    def body(state):
        i, _ = state
        j = qi - 1 - 2 * i
        step([j, j - 1], None)
        return i + 1, alive()

    _, live = lax.while_loop(cond, body, (jnp.int32(0), alive()))

    @pl.when(jnp.logical_and(qi % 2 == 1, live))
    def _():
        step([0], None)

    for g in range(g4):
        o_ref[:, g * SB_HEAD_DIM:(g + 1) * SB_HEAD_DIM] = acc[g * t:(g + 1) * t, :].astype(o_ref.dtype)


def sb_prompt(proj, batch, seq):
    nq = seq // ROW_TILE
    qw = SB_Q_PER_KV * SB_HEAD_DIM
    q0 = (A_XBC) // qw
    k0 = (A_XBC + SB_HEADS * SB_HEAD_DIM) // SB_HEAD_DIM
    v0 = k0 + SB_KV_HEADS
    return pl.pallas_call(
        _sb_prompt_kernel,
        grid=(batch, SB_KV_HEADS, nq),
        in_specs=[pl.BlockSpec((ROW_TILE, qw), lambda b, h, i: (b * nq + i, q0 + h)),
                  pl.BlockSpec((seq, SB_HEAD_DIM), lambda b, h, i: (b, k0 + h)),
                  pl.BlockSpec((seq, SB_HEAD_DIM), lambda b, h, i: (b, v0 + h))],
        out_specs=pl.BlockSpec((ROW_TILE, qw), lambda b, h, i: (b * nq + i, h)),
        out_shape=jax.ShapeDtypeStruct((batch * seq, SB_HEADS * SB_HEAD_DIM), BF16),
        scratch_shapes=[pltpu.VMEM((SB_Q_PER_KV * ROW_TILE, SB_HEAD_DIM), F32),
                        pltpu.VMEM((SB_Q_PER_KV * ROW_TILE, LANES), F32)],
        compiler_params=_params(("parallel", "parallel", "arbitrary")),
        name="sb_prompt",
    )(proj, proj, proj)


def _sb_sample_kernel(*refs, pages_per_step, t, first):
    if first:
        pt_ref, q_ref, kn_ref, vn_ref = refs[:4]
        refs = refs[4:]
    else:
        pt_ref, dead_ref, q_ref, acc_in, carry_in = refs[:5]
        refs = refs[5:]
    k_refs = refs[:pages_per_step]
    v_refs = refs[pages_per_step:2 * pages_per_step]
    if first:
        acc_out, carry_out, dead_out, acc, carry, knp, vnp = refs[2 * pages_per_step:]
    else:
        o_ref, acc, carry = refs[2 * pages_per_step:]
    rows = q_ref.shape[1]
    rk = rows // SB_KV_HEADS
    later01 = _later_matrix()
    qs = (q_ref[0] * SB_SCALE).astype(BF16)

    heads = range(SB_KV_HEADS)
    hrow = lambda h: slice(h * rk, (h + 1) * rk)

    def blocks(zs, vfns, vis):
        ws, c = _sb_weights(zs, vis, carry[...], later01)
        carry[...] = c
        ws = [w.astype(BF16) for w in ws]
        outs = []
        for h in range(0, SB_KV_HEADS, 2):
            wh = jnp.concatenate([w[h * rk:(h + 2) * rk] for w in ws], axis=1)
            vh = jnp.concatenate([jnp.concatenate([vfn(h), vfn(h + 1)], axis=1) for vfn in vfns], axis=0)
            o2 = _dot(wh, vh)
            outs += [o2[0:rk, 0:SB_HEAD_DIM], o2[rk:2 * rk, SB_HEAD_DIM:]]
        acc[...] += jnp.concatenate(outs, axis=0)

    def new_tokens():
        knp[...] = jnp.zeros(knp.shape, F32)
        vnp[...] = jnp.zeros(vnp.shape, F32)
        knp[0:t, :] = kn_ref[...]
        vnp[0:t, :] = vn_ref[...]
        row_t = lax.broadcasted_iota(jnp.int32, (rows, LANES), 0) % t
        col = lax.broadcasted_iota(jnp.int32, (rows, LANES), 1)
        hd = lambda ref: (lambda h: ref[:, h * SB_HEAD_DIM:(h + 1) * SB_HEAD_DIM].astype(BF16))
        z_new = jnp.concatenate([_dot_nt(qs[hrow(h)], hd(knp)(h)) for h in heads], axis=0)
        blocks([z_new], [hd(vnp)], col < row_t)

    def cached_pages():
        head = lambda ref: (lambda h: ref[pl.ds(h, PAGE_SIZE, stride=SB_KV_HEADS), :].astype(BF16))
        q_t = (q_ref[0] * SB_SCALE).T.astype(BF16)
        col_head = lax.broadcasted_iota(jnp.int32, (SB_HEAD_DIM, rows), 1) // rk
        q_bd = jnp.concatenate([jnp.where(col_head == h, q_t, jnp.zeros((), BF16)) for h in heads], axis=0)
        k_cat = lambda r: jnp.concatenate([head(r)(h) for h in heads], axis=1)
        zs = []
        half = (pages_per_step + 1) // 2
        for i0 in range(0, pages_per_step, half):
            part = k_refs[i0:i0 + half]
            z_t = _dot(jnp.concatenate([k_cat(r) for r in part], axis=0), q_bd)
            zs += [z_t[i * PAGE_SIZE:(i + 1) * PAGE_SIZE, :].T for i in range(len(part))]
        blocks(zs, [head(r) for r in v_refs], None)

    if first:
        acc[...] = jnp.zeros(acc.shape, F32)
        carry[...] = jnp.zeros(carry.shape, F32)
        new_tokens()
        cached_pages()
        acc_out[0] = acc[...]
        carry_out[0] = carry[...]
        dead_out[...] = jnp.full(dead_out.shape, jnp.where(jnp.max(carry[...]) > SB_DEAD, 0, 1), jnp.int32)
    else:
        b = pl.program_id(0)
        p = pl.program_id(1)

        @pl.when(p == 0)
        def _():
            acc[...] = acc_in[0]
            carry[...] = carry_in[0]

        @pl.when(jnp.logical_and(dead_ref[b] == 0, jnp.max(carry[...]) > SB_DEAD))
        def _():
            cached_pages()

        @pl.when(p == pl.num_programs(1) - 1)
        def _():
            o_ref[0] = acc[...].astype(o_ref.dtype)


def sb_sample(q_rows, proj, cache_k, cache_v, layer, page_table, t, first_pages=4, pages_per_step=20):
    nseq, n_pages = page_table.shape
    kw = SB_KV_HEADS * SB_HEAD_DIM
    k0 = (A_XBC + SB_HEADS * SB_HEAD_DIM) // kw
    rows = q_rows.shape[1]
    page_rows = PAGE_SIZE * SB_KV_HEADS
    cache_k = cache_k.reshape(cache_k.shape[:2] + (page_rows, SB_HEAD_DIM))
    cache_v = cache_v.reshape(cache_v.shape[:2] + (page_rows, SB_HEAD_DIM))
    g = min(first_pages, n_pages)
    row_spec = lambda: pl.BlockSpec((1, rows, SB_HEAD_DIM), lambda b, *_: (b, 0, 0))
    page_block = (None, None, page_rows, SB_HEAD_DIM)

    first_page = lambda i: pl.BlockSpec(page_block, lambda b, pt: (layer, pt[b, n_pages - 1 - i], 0, 0))
    acc, carry, dead = pl.pallas_call(
        functools.partial(_sb_sample_kernel, pages_per_step=g, t=t, first=True),
        grid_spec=pltpu.PrefetchScalarGridSpec(
            num_scalar_prefetch=1,
            grid=(nseq,),
            in_specs=[row_spec(),
                      pl.BlockSpec((t, kw), lambda b, pt: (b, k0)),
                      pl.BlockSpec((t, kw), lambda b, pt: (b, k0 + 1))]
                     + [first_page(i) for i in range(g)] + [first_page(i) for i in range(g)],
            out_specs=[row_spec(), row_spec(), pl.BlockSpec((1, 8, LANES), lambda b, pt: (b, 0, 0))],
            scratch_shapes=[pltpu.VMEM((rows, SB_HEAD_DIM), F32), pltpu.VMEM((rows, LANES), F32),
                            pltpu.VMEM((PAGE_SIZE, kw), F32), pltpu.VMEM((PAGE_SIZE, kw), F32)]),
        out_shape=[jax.ShapeDtypeStruct((nseq, rows, SB_HEAD_DIM), F32),
                   jax.ShapeDtypeStruct((nseq, rows, LANES), F32),
                   jax.ShapeDtypeStruct((nseq, 8, LANES), jnp.int32)],
        compiler_params=_params(("arbitrary",)),
        name="sb_sample_recent",
    )(page_table, q_rows, proj, proj, *([cache_k] * g), *([cache_v] * g))
    rem = n_pages - g
    if rem == 0:
        return acc.astype(BF16)
    pps = max(d for d in range(1, min(pages_per_step, rem) + 1) if rem % d == 0)

    def old_page(i):
        return pl.BlockSpec(page_block, lambda b, p, pt, dn: (
            layer, jnp.where(dn[b] > 0, 0, pt[b, rem - 1 - (p * pps + i)]), 0, 0))

    older = pl.pallas_call(
        functools.partial(_sb_sample_kernel, pages_per_step=pps, t=t, first=False),
        grid_spec=pltpu.PrefetchScalarGridSpec(
            num_scalar_prefetch=2,
            grid=(nseq, rem // pps),
            in_specs=[row_spec(), row_spec(), row_spec()]
                     + [old_page(i) for i in range(pps)] + [old_page(i) for i in range(pps)],
            out_specs=row_spec(),
            scratch_shapes=[pltpu.VMEM((rows, SB_HEAD_DIM), F32), pltpu.VMEM((rows, LANES), F32)]),
        out_shape=jax.ShapeDtypeStruct((nseq, rows, SB_HEAD_DIM), BF16),
        compiler_params=_params(("arbitrary", "arbitrary")),
        name="sb_sample_older",
    )
    dead = dead[:, 0, 0]
    return lax.cond(jnp.all(dead > 0),
                    lambda: acc.astype(BF16),
                    lambda: older(page_table, dead, q_rows, acc, carry, *([cache_k] * pps), *([cache_v] * pps)))


def _layer_a(xp, xs, batch, seq, nseq, t, cache_k, cache_v, layer, state_ssm, state_conv, page_table,
             g_mix, g_ffn, w_in, w_conv, b_conv, dt_bias, a_log, d_skip, g_ssd, w_out, ffn_w):
    w_main = jnp.concatenate([w_in[:, :A_XBC], w_in[:, A_DT:]], axis=1).astype(BF16)
    w_dt = jnp.pad(w_in[:, A_XBC:A_DT], ((0, 0), (0, LANES - SSD_HEADS))).astype(BF16)
    w_out = w_out.astype(BF16)
    gn2 = 2 * SSD_GROUPS * SSD_STATE
    qcol = A_XBC
    kcol = qcol + SB_HEADS * SB_HEAD_DIM
    vcol = kcol + SB_KV_HEADS * SB_HEAD_DIM

    pp = norm_matmul(xp, g_mix, w_main)
    dp = norm_matmul(xp, g_mix, w_dt)
    y_p, ssm_p = ssd_prompt(pp, dp, batch, seq, w_conv, b_conv, dt_bias, a_log, d_skip, g_ssd)
    o_p = sb_prompt(pp, batch, seq)
    xp = out_proj(y_p, o_p, w_out, xp)
    xp = ffn(xp, g_ffn, *ffn_w)
    sbk_p = pp[:, kcol:vcol].reshape(batch, seq, SB_KV_HEADS, SB_HEAD_DIM)
    sbv_p = pp[:, vcol:].reshape(batch, seq, SB_KV_HEADS, SB_HEAD_DIM)
    conv_p = pp.reshape(batch, seq, -1)[:, seq - (SSD_CONV - 1):, A_Z:A_Z + SSD_CONV_DIM]
    ssm_p = ssm_p.reshape(batch, SSD_HEADS, SSD_HEAD_DIM, SSD_STATE)

    ps = norm_matmul(xs, g_mix, w_main)
    ds = norm_matmul(xs, g_mix, w_dt)
    y_s, conv_s, ssm_s = ssd_sample(ps, ds, nseq, t, state_conv, state_ssm.reshape(nseq, SSD_D_INNER, SSD_STATE),
                                    w_conv, b_conv, dt_bias, a_log, d_skip, g_ssd)
    q_rows = ps[:, qcol:kcol].reshape(nseq, t, SB_KV_HEADS, SB_Q_PER_KV, SB_HEAD_DIM)
    q_rows = q_rows.transpose(0, 2, 3, 1, 4).reshape(nseq, SB_HEADS * t, SB_HEAD_DIM)
    o_s = sb_sample(q_rows, ps, cache_k, cache_v, layer, page_table, t)
    o_s = o_s.reshape(nseq, SB_KV_HEADS, SB_Q_PER_KV, t, SB_HEAD_DIM).transpose(0, 3, 1, 2, 4)
    o_s = o_s.reshape(nseq * t, SB_HEADS * SB_HEAD_DIM)
    xs = out_proj(y_s, o_s, w_out, xs)
    xs = ffn(xs, g_ffn, *ffn_w)
    sbk_s = ps[:, kcol:vcol].reshape(nseq, t, SB_KV_HEADS, SB_HEAD_DIM)
    sbv_s = ps[:, vcol:].reshape(nseq, t, SB_KV_HEADS, SB_HEAD_DIM)
    ssm_s = ssm_s.reshape(nseq, SSD_HEADS, SSD_HEAD_DIM, SSD_STATE)
    return xp, xs, (sbk_p, sbv_p, ssm_p, conv_p), (sbk_s, sbv_s, ssm_s, conv_s)


MLA_PREP_ROWS = 256
C_MAIN = 2 * CMLP_DIM + MLA_Q_RANK + MLA_KV_RANK


def _half_head_ones():
    a = lax.broadcasted_iota(jnp.int32, (LANES, LANES), 0) // MLA_ROPE
    b = lax.broadcasted_iota(jnp.int32, (LANES, LANES), 1) // MLA_ROPE
    return (a == b).astype(BF16)


def _rope_norm(x, g, cos, sin, ones64):
    ms = _dot01_right(x * x, ones64) * (1.0 / MLA_ROPE)
    xn = x * lax.rsqrt(ms + EPS) * g
    lane = lax.broadcasted_iota(jnp.int32, x.shape, 1) % MLA_ROPE
    rot = jnp.where(lane < MLA_ROPE // 2, -pltpu.roll(xn, LANES - MLA_ROPE // 2, 1), pltpu.roll(xn, MLA_ROPE // 2, 1))
    return xn * cos + rot * sin


def _mla_prep_kernel(cq_ref, ckv_ref, kpe_ref, cos_ref, sin_ref, gcq_ref, gckv_ref, gqn_ref, gqr_ref, gkr_ref,
                     wqn_ref, wqr_ref, *rest, with_kv, with_scale):
    if with_kv:
        gkn_ref, wuk_ref, wuv_ref, qn_ref, qp_ref, ckvn_ref, kper_ref, kn_ref, vv_ref, kp2_ref = rest
    elif with_scale:
        wuk_ref, qn_ref, qp_ref, ckvn_ref, kper_ref, rsn_ref = rest
    else:
        qn_ref, qp_ref, ckvn_ref, kper_ref = rest
    ones64 = _half_head_ones()
    cos, sin = cos_ref[...], sin_ref[...]
    cqn = _rms(cq_ref[...], gcq_ref[...]).astype(BF16)
    qn = _dot(cqn, wqn_ref[...])
    qp = _dot(cqn, wqr_ref[...])
    for h in range(MLA_HEADS):
        cs = slice(h * MLA_NOPE, (h + 1) * MLA_NOPE)
        qn_ref[:, cs] = (_rms(qn[:, cs], gqn_ref[:, cs]) * MLA_SCALE).astype(qn_ref.dtype)
    for b in range(MLA_HEADS * MLA_ROPE // LANES):
        cs = slice(b * LANES, (b + 1) * LANES)
        qp_ref[:, cs] = (_rope_norm(qp[:, cs], gqr_ref[...], cos, sin, ones64) * MLA_SCALE).astype(qp_ref.dtype)
    ckvn = _rms(ckv_ref[...], gckv_ref[...])
    ckvn_ref[...] = ckvn
    kper = _rope_norm(kpe_ref[...], gkr_ref[...], cos, sin, ones64)
    kper_ref[...] = kper
    if with_kv:
        cb = ckvn.astype(BF16)
        kraw = _dot(cb, wuk_ref[...])
        for h in range(MLA_HEADS):
            cs = slice(h * MLA_NOPE, (h + 1) * MLA_NOPE)
            kn_ref[:, cs] = _rms(kraw[:, cs], gkn_ref[...]).astype(kn_ref.dtype)
        vv_ref[...] = _dot(cb, wuv_ref[...]).astype(vv_ref.dtype)
        kp2_ref[...] = (kper + pltpu.roll(kper, MLA_ROPE, 1)).astype(kp2_ref.dtype)
    elif with_scale:
        kraw = _dot(ckvn.astype(BF16), wuk_ref[...])
        lane = lax.broadcasted_iota(jnp.int32, (kraw.shape[0], LANES), 1)
        rsn = jnp.zeros((kraw.shape[0], LANES), F32)
        for h in range(MLA_HEADS):
            kh = kraw[:, h * MLA_NOPE:(h + 1) * MLA_NOPE]
            rsn = jnp.where(lane == h, lax.rsqrt(jnp.mean(kh * kh, axis=-1, keepdims=True) + EPS), rsn)
        rsn_ref[...] = rsn


def mla_prep(proj, kpe_raw, cos, sin, g_cq, g_ckv, g_qn_row, g_qr, g_kr, wq_nope, wq_rope, kv_weights=None,
             key_scale_weight=None):
    m = proj.shape[0]
    tr = min(MLA_PREP_ROWS, m)
    ncos = cos.shape[0] // tr
    c0 = 2 * CMLP_DIM // MLA_Q_RANK
    hq = MLA_HEADS * MLA_NOPE
    hr = MLA_HEADS * MLA_ROPE
    with_kv = kv_weights is not None
    with_scale = key_scale_weight is not None and not with_kv
    const = lambda shape: pl.BlockSpec(shape, lambda i: (0, 0))
    row = lambda w: pl.BlockSpec((tr, w), lambda i: (i, 0))
    pad_row = lambda v: jnp.concatenate([v, jnp.zeros((LANES - v.shape[0],), F32)]).reshape(1, LANES)
    in_specs = [pl.BlockSpec((tr, MLA_Q_RANK), lambda i: (i, c0)),
                pl.BlockSpec((tr, MLA_KV_RANK), lambda i: (i, c0 + 1)),
                row(LANES),
                pl.BlockSpec((tr, LANES), lambda i: (i % ncos, 0)),
                pl.BlockSpec((tr, LANES), lambda i: (i % ncos, 0)),
                const((1, MLA_Q_RANK)), const((1, MLA_KV_RANK)), const((1, hq)), const((1, LANES)), const((1, LANES)),
                const((MLA_Q_RANK, hq)), const((MLA_Q_RANK, hr))]
    args = [proj, proj, kpe_raw, cos, sin, g_cq.reshape(1, -1), g_ckv.reshape(1, -1), g_qn_row.reshape(1, hq),
            jnp.tile(g_qr, LANES // MLA_ROPE).reshape(1, LANES), pad_row(g_kr), wq_nope, wq_rope]
    out_specs = [row(hq), row(hr), row(MLA_KV_RANK), row(LANES)]
    out_shape = [jax.ShapeDtypeStruct((m, hq), BF16), jax.ShapeDtypeStruct((m, hr), BF16),
                 jax.ShapeDtypeStruct((m, MLA_KV_RANK), F32), jax.ShapeDtypeStruct((m, LANES), F32)]
    if with_kv:
        g_kn, w_uk, w_uv = kv_weights
        in_specs += [const((1, MLA_NOPE)), const((MLA_KV_RANK, hq)), const((MLA_KV_RANK, MLA_HEADS * MLA_V))]
        args += [g_kn.reshape(1, -1), w_uk, w_uv]
        out_specs += [row(hq), row(MLA_HEADS * MLA_V), row(LANES)]
        out_shape += [jax.ShapeDtypeStruct((m, hq), BF16), jax.ShapeDtypeStruct((m, MLA_HEADS * MLA_V), BF16),
                      jax.ShapeDtypeStruct((m, LANES), BF16)]
    elif with_scale:
        in_specs += [const((MLA_KV_RANK, hq))]
        args += [key_scale_weight]
        out_specs += [row(LANES)]
        out_shape += [jax.ShapeDtypeStruct((m, LANES), F32)]
    return pl.pallas_call(
        functools.partial(_mla_prep_kernel, with_kv=with_kv, with_scale=with_scale),
        grid=(m // tr,),
        in_specs=in_specs, out_specs=out_specs, out_shape=out_shape,
        compiler_params=_params(("parallel",)),
        name="mla_prep",
    )(*args)


def _softmax_update(s, v, m_ref, l_ref, acc_ref):
    p, alpha = _softmax_stats(s, m_ref, l_ref)
    _softmax_accumulate(p, alpha, v, acc_ref)


def _softmax_stats(s, m_ref, l_ref):
    m_old = m_ref[...]
    m_new = jnp.maximum(m_old, jnp.max(s, axis=-1, keepdims=True))
    alpha = jnp.exp(m_old - m_new)
    p = jnp.exp(s - m_new[:, :1])
    l_ref[...] = alpha * l_ref[...] + jnp.sum(p, axis=-1, keepdims=True)
    m_ref[...] = m_new
    return p.astype(BF16), alpha[:, :1]


def _softmax_accumulate(p, alpha, v, acc_ref):
    acc_ref[...] = alpha * acc_ref[...] + _dot(p, v)


def _mla_prompt_kernel(qn_ref, qp_ref, kn_ref, kp_ref, vv_ref, o_ref, m_s, l_s, acc_s, *, tq, nh):
    qi = pl.program_id(2)
    lane_head = lax.broadcasted_iota(jnp.int32, (tq, LANES), 1) // MLA_ROPE
    row = lax.broadcasted_iota(jnp.int32, (tq, tq), 0)
    col = lax.broadcasted_iota(jnp.int32, (tq, tq), 1)
    m_s[...] = jnp.full(m_s.shape, -jnp.inf, F32)
    l_s[...] = jnp.zeros(l_s.shape, F32)
    acc_s[...] = jnp.zeros(acc_s.shape, F32)

    def process(j, mask):
        ks = pl.ds(pl.multiple_of(j * tq, tq), tq)
        kp = kp_ref[ks, :]
        head_cols = [slice(hh * MLA_NOPE, (hh + 1) * MLA_NOPE) for hh in range(nh)]
        ones = jnp.ones((tq, LANES), BF16)
        ss = []
        for hh, cs in enumerate(head_cols):
            qp = qp_ref[:, (hh // 2) * LANES:(hh // 2 + 1) * LANES]
            q2 = jnp.where(lane_head == hh % 2, qp, jnp.zeros((), qp.dtype))
            s = _dot_nt(jnp.concatenate([qn_ref[:, cs], q2], axis=1),
                        jnp.concatenate([kn_ref[ks, cs], kp], axis=1))
            ss.append(s if mask is None else jnp.where(mask, s, -jnp.inf))
        pa = []
        for hh, s in enumerate(ss):
            m_old = m_s[hh]
            m_new = jnp.maximum(m_old, jnp.max(s, axis=-1, keepdims=True))
            m_s[hh] = m_new
            pa.append((jnp.exp(s - m_new[:, :1]).astype(BF16), jnp.exp(m_old - m_new)))
        for hh, cs in enumerate(head_cols):
            pv = _dot(pa[hh][0], jnp.concatenate([vv_ref[ks, cs], ones], axis=1))
            acc_s[hh] = pa[hh][1] * acc_s[hh] + pv[:, :MLA_V]
            l_s[hh] = pa[hh][1] * l_s[hh] + pv[:, MLA_V:]

    def body(j, _):
        process(j, None)
        return 0

    lax.fori_loop(0, qi, body, 0)
    process(qi, col <= row)
    for hh in range(nh):
        cs = slice(hh * MLA_NOPE, (hh + 1) * MLA_NOPE)
        o_ref[:, cs] = (acc_s[hh] / l_s[hh][:, :1]).astype(o_ref.dtype)


def mla_prompt(qn, qp, kn, kp2, vv, batch, seq, tq=256, nh=4):
    tq = min(tq, seq)
    nq = seq // tq
    hw = nh * MLA_NOPE
    rw = nh * MLA_ROPE
    return pl.pallas_call(
        functools.partial(_mla_prompt_kernel, tq=tq, nh=nh),
        grid=(batch, MLA_HEADS // nh, nq),
        in_specs=[pl.BlockSpec((tq, hw), lambda b, h, i: (b * nq + i, h)),
                  pl.BlockSpec((tq, rw), lambda b, h, i: (b * nq + i, h)),
                  pl.BlockSpec((seq, hw), lambda b, h, i: (b, h)),
                  pl.BlockSpec((seq, LANES), lambda b, h, i: (b, 0)),
                  pl.BlockSpec((seq, hw), lambda b, h, i: (b, h))],
        out_specs=pl.BlockSpec((tq, hw), lambda b, h, i: (b * nq + i, h)),
        out_shape=jax.ShapeDtypeStruct((batch * seq, MLA_HEADS * MLA_V), BF16),
        scratch_shapes=[pltpu.VMEM((nh, tq, LANES), F32), pltpu.VMEM((nh, tq, LANES), F32),
                        pltpu.VMEM((nh, tq, MLA_V), F32)],
        compiler_params=_params(("parallel", "parallel", "arbitrary")),
        name="mla_prompt",
    )(qn, qp, kn, kp2, vv)


def _head_matmul_kernel(a_ref, w_ref, o_ref):
    a = a_ref[...]
    a = a.reshape(-1, a.shape[-1]).astype(BF16)
    o_ref[...] = _dot(a, w_ref[0]).reshape(o_ref.shape).astype(o_ref.dtype)


def absorb_queries(qg, w_uk_t, nseq, t):
    return pl.pallas_call(
        _head_matmul_kernel,
        grid=(MLA_HEADS,),
        in_specs=[pl.BlockSpec((nseq * t, MLA_NOPE), lambda h: (0, h)),
                  pl.BlockSpec((1, MLA_NOPE, MLA_KV_RANK), lambda h: (h, 0, 0))],
        out_specs=pl.BlockSpec((nseq, None, t, MLA_KV_RANK), lambda h: (0, h, 0, 0)),
        out_shape=jax.ShapeDtypeStruct((nseq, MLA_HEADS, t, MLA_KV_RANK), F32),
        compiler_params=_params(("parallel",)),
        name="absorb_queries",
    )(qg, w_uk_t)


def project_latent_out(o_lat, w_uv_h, nseq, t):
    return pl.pallas_call(
        _head_matmul_kernel,
        grid=(MLA_HEADS,),
        in_specs=[pl.BlockSpec((nseq, None, t, MLA_KV_RANK), lambda h: (0, h, 0, 0)),
                  pl.BlockSpec((1, MLA_KV_RANK, MLA_V), lambda h: (h, 0, 0))],
        out_specs=pl.BlockSpec((nseq * t, MLA_V), lambda h: (0, h)),
        out_shape=jax.ShapeDtypeStruct((nseq * t, MLA_HEADS * MLA_V), BF16),
        compiler_params=_params(("parallel",)),
        name="project_latent_out",
    )(o_lat, w_uv_h)


MLA_BLOCK_PAGES = 8


def _mla_sample_kernel(pt_ref, qa_ref, qp_ref, cn_ref, kn_ref, rsn_ref, wuk_ref, *refs, pages_per_step, t):
    c_refs = refs[:pages_per_step]
    k_refs = refs[pages_per_step:2 * pages_per_step]
    o_ref, lhs, m_s, l_s, acc_s, cpad, kpad = refs[2 * pages_per_step:]
    b = pl.program_id(0)
    p = pl.program_id(1)
    nk = MLA_HEADS * MLA_NOPE
    rows = MLA_HEADS * t

    @pl.when(jnp.logical_and(b == 0, p == 0))
    def _():
        lhs[0:nk, :] = wuk_ref[...]

    def front(c):
        keys = c.shape[0]
        kt = _dot_nt(lhs[...], c)
        k2 = kt[0:nk, :]
        k2 = (k2 * k2).reshape(MLA_HEADS, MLA_NOPE // 8, 8, keys).sum(axis=1)
        for sh in (4, 2, 1):
            k2 = k2 + pltpu.roll(k2, sh, 1)
        rs = lax.rsqrt(k2 * (1.0 / MLA_NOPE) + EPS).reshape(rows, keys)
        return kt[nk:nk + rows, :] * rs

    def back(s, c, kpe_t, mask):
        s = s + _dot(qp_ref[0].astype(BF16), kpe_t)
        if mask is not None:
            s = jnp.where(mask, s, -jnp.inf)
        _softmax_update(s, c, m_s, l_s, acc_s)

    @pl.when(p == 0)
    def _():
        lhs[nk:nk + rows, :] = qa_ref[0].reshape(rows, MLA_KV_RANK).astype(BF16)
        m_s[...] = jnp.full(m_s.shape, -jnp.inf, F32)
        l_s[...] = jnp.zeros(l_s.shape, F32)
        acc_s[...] = jnp.zeros(acc_s.shape, F32)
        cpad[...] = jnp.zeros(cpad.shape, F32)
        kpad[...] = jnp.zeros(kpad.shape, F32)
        cpad[0:t, :] = cn_ref[...]
        kpad[0:t, :] = kn_ref[...]
        row_t = lax.broadcasted_iota(jnp.int32, (rows, LANES), 0) % t
        col = lax.broadcasted_iota(jnp.int32, (rows, LANES), 1)
        cn = cpad[...].astype(BF16)
        rs_new = jnp.broadcast_to(rsn_ref[0][:, None, :], (MLA_HEADS, t, LANES)).reshape(rows, LANES)
        back(_dot_nt(lhs[nk:nk + rows, :], cn) * rs_new, cn, kpad[...].T[0:MLA_ROPE, :].astype(BF16), col <= row_t)

    bp = min(MLA_BLOCK_PAGES, pages_per_step)
    latent = lambda i: jnp.concatenate([c_refs[i + k][...] for k in range(bp)], axis=0).astype(BF16)
    rotary = lambda i: jnp.concatenate([k_refs[i + k][...] for k in range(bp)], axis=1).astype(BF16)
    starts = list(range(0, pages_per_step, bp))
    pending = front(latent(starts[0]))
    for n, i in enumerate(starts):
        nxt = front(latent(starts[n + 1])) if n + 1 < len(starts) else None
        back(pending, latent(i), rotary(i), None)
        pending = nxt

    @pl.when(p == pl.num_programs(1) - 1)
    def _():
        o_ref[0] = (acc_s[...] / l_s[:, :1]).reshape(o_ref.shape[1:])


def mla_sample(q_abs, q_pe, ckv_new, kpe_new, rs_new, w_uk_t2, cache_ckv, cache_kpe, layer, page_table, t,
               pages_per_step=32):
    nseq, n_pages = page_table.shape
    pps = min(pages_per_step, n_pages)
    nsteps = n_pages // pps
    rows = MLA_HEADS * t
    nk = MLA_HEADS * MLA_NOPE
    cache_kpe_t = jnp.swapaxes(cache_kpe, 2, 3)

    def page_spec(shape, i):
        return pl.BlockSpec((None, None) + shape, lambda b, p, pt: (layer, pt[b, p * pps + i], 0, 0))

    grid_spec = pltpu.PrefetchScalarGridSpec(
        num_scalar_prefetch=1,
        grid=(nseq, nsteps),
        in_specs=[pl.BlockSpec((1, MLA_HEADS, t, MLA_KV_RANK), lambda b, p, pt: (b, 0, 0, 0)),
                  pl.BlockSpec((1, rows, MLA_ROPE), lambda b, p, pt: (b, 0, 0)),
                  pl.BlockSpec((t, MLA_KV_RANK), lambda b, p, pt: (b, 0)),
                  pl.BlockSpec((t, LANES), lambda b, p, pt: (b, 0)),
                  pl.BlockSpec((1, MLA_HEADS, LANES), lambda b, p, pt: (b, 0, 0)),
                  pl.BlockSpec((nk, MLA_KV_RANK), lambda b, p, pt: (0, 0))]
                 + [page_spec((PAGE_SIZE, MLA_KV_RANK), i) for i in range(pps)]
                 + [page_spec((MLA_ROPE, PAGE_SIZE), i) for i in range(pps)],
        out_specs=pl.BlockSpec((1, MLA_HEADS, t, MLA_KV_RANK), lambda b, p, pt: (b, 0, 0, 0)),
        scratch_shapes=[pltpu.VMEM((nk + rows, MLA_KV_RANK), BF16),
                        pltpu.VMEM((rows, LANES), F32), pltpu.VMEM((rows, LANES), F32),
                        pltpu.VMEM((rows, MLA_KV_RANK), F32),
                        pltpu.VMEM((PAGE_SIZE, MLA_KV_RANK), F32), pltpu.VMEM((PAGE_SIZE, LANES), F32)],
    )
    return pl.pallas_call(
        functools.partial(_mla_sample_kernel, pages_per_step=pps, t=t),
        grid_spec=grid_spec,
        out_shape=jax.ShapeDtypeStruct((nseq, MLA_HEADS, t, MLA_KV_RANK), F32),
        compiler_params=_params(("arbitrary", "arbitrary")),
        name="mla_sample",
    )(page_table, q_abs, q_pe, ckv_new, kpe_new, rs_new, w_uk_t2, *([cache_ckv] * pps), *([cache_kpe_t] * pps))


def _gelu(x):
    return 0.5 * x * (1.0 + jnp.tanh(0.7978845608028654 * (x + 0.044715 * (x * x * x))))


def _cmlp_kernel(u_ref, v_ref, gv_ref, bv_ref, w_ref, bs_ref, o_ref, *rest, seg):
    r = u_ref.shape[0]
    u = _gelu(u_ref[...])
    v = _gelu(v_ref[...])
    mu = jnp.mean(v, axis=-1, keepdims=True)
    vc = v - mu
    var = jnp.mean(vc * vc, axis=-1, keepdims=True)
    vn = vc * lax.rsqrt(var + EPS) * gv_ref[...] + bv_ref[...]
    if rest:
        rest[0][...] = vn
    vb = vn.astype(BF16)
    row = lax.broadcasted_iota(jnp.int32, (r, r), 0)
    col = lax.broadcasted_iota(jnp.int32, (r, r), 1)
    keep = jnp.logical_and(col <= row, (row // seg) == (col // seg))
    gw = CMLP_DIM // CMLP_GROUPS
    for g in range(CMLP_GROUPS):
        cs = slice(g * gw, (g + 1) * gw)
        w = jnp.where(keep, w_ref[g], 0.0).astype(BF16)
        mixed = _dot(w, vb[:, cs]) + bs_ref[:, g:g + 1]
        o_ref[:, cs] = (u[:, cs] * mixed).astype(o_ref.dtype)


def chunk_mlp(proj, g_v, b_v, w_mix, b_mix, seg, emit_v):
    m = proj.shape[0]
    d = CMLP_DIM
    const2 = lambda shape: pl.BlockSpec(shape, lambda i: (0, 0))
    out_specs = [pl.BlockSpec((ROW_TILE, d), lambda i: (i, 0))]
    out_shape = [jax.ShapeDtypeStruct((m, d), BF16)]
    if emit_v:
        out_specs.append(pl.BlockSpec((ROW_TILE, d), lambda i: (i, 0)))
        out_shape.append(jax.ShapeDtypeStruct((m, d), F32))
    return pl.pallas_call(
        functools.partial(_cmlp_kernel, seg=seg),
        grid=(m // ROW_TILE,),
        in_specs=[pl.BlockSpec((ROW_TILE, d), lambda i: (i, 0)),
                  pl.BlockSpec((ROW_TILE, d), lambda i: (i, 1)),
                  const2((1, d)), const2((1, d)),
                  pl.BlockSpec((CMLP_GROUPS, ROW_TILE, ROW_TILE), lambda i: (0, 0, 0)),
                  const2((ROW_TILE, CMLP_GROUPS))],
        out_specs=out_specs, out_shape=out_shape,
        compiler_params=_params(("parallel",)),
        name="chunk_mlp",
    )(proj, proj, g_v.reshape(1, d), b_v.reshape(1, d), w_mix, b_mix)


def _rope_tables(pos):
    half = MLA_ROPE // 2
    inv = ROPE_THETA ** (-jnp.arange(half, dtype=F32) / half)
    ang = pos.astype(F32)[:, None] * inv[None, :]
    reps = LANES // half
    return jnp.tile(jnp.cos(ang), (1, reps)), jnp.tile(jnp.sin(ang), (1, reps))


def _layer_c(xp, xs, batch, seq, nseq, t, cache_ckv, cache_kpe, layer, page_table, g_mix, g_ffn, w_in, g_cq, g_ckv,
             w_uq, g_qn, g_qr, g_kn, g_kr, w_uk, w_uv, g_v, b_v, w_s, b_s, w_out, ffn_w):
    past = page_table.shape[1] * PAGE_SIZE
    w_main = jnp.concatenate([w_in[:, C_KPE:], w_in[:, :C_CKV]], axis=1).astype(BF16)
    w_kpe = jnp.pad(w_in[:, C_CKV:C_KPE], ((0, 0), (0, LANES - MLA_ROPE))).astype(BF16)
    wq = w_uq.reshape(MLA_Q_RANK, MLA_HEADS, MLA_NOPE + MLA_ROPE)
    wq_nope = wq[:, :, :MLA_NOPE].reshape(MLA_Q_RANK, -1).astype(BF16)
    wq_rope = wq[:, :, MLA_NOPE:].reshape(MLA_Q_RANK, -1).astype(BF16)
    w_uk2 = w_uk.reshape(MLA_KV_RANK, -1).astype(BF16)
    w_uv2 = w_uv.reshape(MLA_KV_RANK, -1).astype(BF16)
    w_uk_t = w_uk.transpose(1, 2, 0).astype(BF16)
    w_uv_h = w_uv.transpose(1, 0, 2).astype(BF16)
    w_out = w_out.astype(BF16)
    ones_row = jnp.ones((MLA_HEADS * MLA_NOPE,), F32)

    pp = norm_matmul(xp, g_mix, w_main)
    kp = norm_matmul(xp, g_mix, w_kpe)
    cos_p, sin_p = _rope_tables(jnp.arange(seq))
    qn, qp, ckv_p, kpe_p, kn, vv, kp2 = mla_prep(pp, kp, cos_p, sin_p, g_cq, g_ckv, jnp.tile(g_qn, MLA_HEADS) * ones_row,
                                                 g_qr, g_kr, wq_nope, wq_rope, (g_kn, w_uk2, w_uv2))
    o_mla_p = mla_prompt(qn, qp, kn, kp2, vv, batch, seq)
    o_cm_p = chunk_mlp(pp, g_v, b_v, w_s, b_s.T, CMLP_CHUNK, False)[0]
    xp = out_proj(o_mla_p, o_cm_p, w_out, xp)
    xp = ffn(xp, g_ffn, *ffn_w)
    ckv_p = ckv_p.reshape(batch, seq, MLA_KV_RANK)
    kpe_p = kpe_p[:, :MLA_ROPE].reshape(batch, seq, MLA_ROPE)

    ps = norm_matmul(xs, g_mix, w_main)
    ks = norm_matmul(xs, g_mix, w_kpe)
    rows_s = nseq * t
    tr = min(MLA_PREP_ROWS, rows_s)
    cos_s, sin_s = _rope_tables(past + (jnp.arange(tr) % t))
    qg, qr, ckv_s, kpe_s, rs_s = mla_prep(ps, ks, cos_s, sin_s, g_cq, g_ckv, jnp.tile(g_qn * g_kn, MLA_HEADS), g_qr,
                                          g_kr, wq_nope, wq_rope, key_scale_weight=w_uk2)
    q_abs = absorb_queries(qg, w_uk_t, nseq, t)
    q_pe = qr.reshape(nseq, t, MLA_HEADS, MLA_ROPE).transpose(0, 2, 1, 3).reshape(nseq, MLA_HEADS * t, MLA_ROPE)
    rs_new = rs_s.reshape(nseq, t, LANES)[:, :, :MLA_HEADS].transpose(0, 2, 1)
    rs_new = jnp.pad(rs_new, ((0, 0), (0, 0), (0, LANES - t)))
    o_lat = mla_sample(q_abs, q_pe.astype(F32), ckv_s, kpe_s, rs_new, w_uk2.T, cache_ckv, cache_kpe, layer,
                       page_table, t)
    o_mla_s = project_latent_out(o_lat, w_uv_h, nseq, t)
    reps = ROW_TILE // t
    w_mix_s = jnp.tile(w_s[:, :t, :t], (1, reps, reps))
    b_mix_s = jnp.tile(b_s[:, :t].T, (reps, 1))
    o_cm_s, cv_s = chunk_mlp(ps, g_v, b_v, w_mix_s, b_mix_s, t, True)
    xs = out_proj(o_mla_s, o_cm_s, w_out, xs)
    xs = ffn(xs, g_ffn, *ffn_w)
    ckv_s = ckv_s.reshape(nseq, t, MLA_KV_RANK)
    kpe_s = kpe_s[:, :MLA_ROPE].reshape(nseq, t, MLA_ROPE)
    return xp, xs, (ckv_p, kpe_p), (ckv_s, kpe_s, cv_s.reshape(nseq, t, CMLP_DIM))


def kernel(x_prompt, x_sample, cache_sb_k, cache_sb_v, cache_mla_ckv, cache_mla_kpe, state_ssm, state_conv, page_table, norm_mix, norm_ffn, w_in_a, w_conv, b_conv, dt_bias, a_log, d_skip, g_ssd, w_out_a, w_in_c, g_cq, g_ckv, w_uq, g_qn, g_qr, g_kn, g_kr, w_uk, w_uv, g_v, b_v, w_s, b_s, w_out_c, w_ffn1, w_ffn2):
    batch, seq, dm = x_prompt.shape
    nseq, t, _ = x_sample.shape
    depth = norm_mix.shape[0]
    xp = x_prompt.reshape(batch * seq, dm)
    xs = x_sample.reshape(nseq * t, dm)
    pa, sa, pc, sc = [], [], [], []
    w1_all = w_ffn1.astype(BF16)
    w2_all = w_ffn2.astype(BF16)
    for l in range(depth):
        i = l // 2
        ffn_w = (w1_all, w2_all, l)
        if l % 2 == 0:
            xp, xs, op, os_ = _layer_a(xp, xs, batch, seq, nseq, t, cache_sb_k, cache_sb_v, i, state_ssm[i], state_conv[i],
                                       page_table, norm_mix[l], norm_ffn[l], w_in_a[i], w_conv[i], b_conv[i], dt_bias[i],
                                       a_log[i], d_skip[i], g_ssd[i], w_out_a[i], ffn_w)
            pa.append(op)
            sa.append(os_)
        else:
            xp, xs, op, os_ = _layer_c(xp, xs, batch, seq, nseq, t, cache_mla_ckv, cache_mla_kpe, i, page_table,
                                       norm_mix[l], norm_ffn[l], w_in_c[i], g_cq[i], g_ckv[i], w_uq[i], g_qn[i], g_qr[i],
                                       g_kn[i], g_kr[i], w_uk[i], w_uv[i], g_v[i], b_v[i], w_s[i], b_s[i], w_out_c[i],
                                       ffn_w)
            pc.append(op)
            sc.append(os_)
    stack = lambda items, k: jnp.stack([it[k] for it in items])
    return (xp.reshape(batch, seq, dm), xs.reshape(nseq, t, dm),
            stack(pa, 0), stack(pa, 1), stack(pa, 2), stack(pa, 3), stack(pc, 0), stack(pc, 1),
            stack(sa, 0), stack(sa, 1), stack(sa, 2), stack(sa, 3), stack(sc, 0), stack(sc, 1), stack(sc, 2))
```
